```python
import math
import jax
import jax.numpy as jnp
from jax import lax
import numpy as np

D_MODEL = 1024
BATCH = 16
SEQ = 256
DEPTH = 2
DEC_BATCH = 2
DEC_SEQ = 4096
PAST_LEN = 512

GRID_W = 64
HEAD_DIM = 64
H_A = D_MODEL // (4 * HEAD_DIM)
H_B = D_MODEL // (2 * HEAD_DIM)
KV_B = H_B // 4
G_B = H_B // KV_B
QKV_SIZES = (2 * H_A * HEAD_DIM, 2 * H_A * HEAD_DIM, 2 * H_A * HEAD_DIM, H_B * HEAD_DIM, KV_B * HEAD_DIM, KV_B * HEAD_DIM)
QKV_WIDTH = sum(QKV_SIZES)
QKV_SPLITS = tuple(int(v) for v in np.cumsum(QKV_SIZES)[:-1])
ATT_OUT = 2 * H_A * HEAD_DIM + H_B * HEAD_DIM
ATTN_SCALE = HEAD_DIM ** -0.5
Q_BLOCK = 128
ROPE_PAIRS = HEAD_DIM // 4
ROPE_THETA = 10000.0
N_ATT = (DEPTH + 1) // 2
N_HY = DEPTH // 2
HY_ORDER = 2
HY_EMB = 33
HY_BANDS = (HY_EMB - 1) // 2
HY_FILTER_W = 64
HY_SHORT = 3
HY_MIN_DECAY = math.log(1e-2) / 1.5
HY_MAX_DECAY = math.log(1e-2) / 0.3
N_EXPERTS = 16
N_GROUPS = 4
EXPERTS_PER_GROUP = N_EXPERTS // N_GROUPS
TOP_K = 2
D_FF_EXPERT = D_MODEL
NORM_EPS = 1e-6
F32 = jnp.float32

kernel_name = 'hybrid_diffattn_gqa_hyena_groupmoe_dit_step'


def rms_norm(x, g):
    xf = x.astype(F32)
    y = xf * lax.rsqrt(jnp.mean(xf * xf, axis=-1, keepdims=True) + NORM_EPS)
    return (y * g.astype(F32)).astype(x.dtype)


def ada_modulation(cvec, w, b):
    m = jax.nn.silu(cvec) @ w + b
    return m.reshape(cvec.shape[0], 6, D_MODEL)


def modulate(x, shift, scale):
    return x * (1.0 + scale) + shift


def axial_rope_tables(n_tok):
    rows = n_tok // GRID_W
    row = jnp.repeat(jnp.arange(rows), GRID_W).astype(F32)
    col = jnp.tile(jnp.arange(GRID_W), rows).astype(F32)
    inv = ROPE_THETA ** (-jnp.arange(ROPE_PAIRS, dtype=F32) / ROPE_PAIRS)
    ang = jnp.concatenate([row[:, None] * inv, col[:, None] * inv], axis=-1)
    return jnp.cos(ang), jnp.sin(ang)


def apply_rope(x, cos, sin):
    b, l, h, _ = x.shape
    xf = x.astype(F32).reshape(b, l, h, 2, 2, ROPE_PAIRS)
    c = cos.reshape(1, l, 1, 2, ROPE_PAIRS)
    s = sin.reshape(1, l, 1, 2, ROPE_PAIRS)
    x1, x2 = xf[..., 0, :], xf[..., 1, :]
    out = jnp.stack([x1 * c - x2 * s, x2 * c + x1 * s], axis=-2)
    return out.reshape(x.shape).astype(x.dtype)


def _query_blocks(q):
    b, l = q.shape[:2]
    return jnp.moveaxis(q.reshape((b, l // Q_BLOCK, Q_BLOCK) + q.shape[2:]), 1, 0)


def _merge_blocks(o):
    o = jnp.moveaxis(o, 0, 1)
    return o.reshape((o.shape[0], o.shape[1] * o.shape[2]) + o.shape[3:])


def diff_attend(q, k, v, lam):
    def block(qb):
        s = jnp.einsum('bqhjd,bkhjd->bhjqk', qb, k).astype(F32) * ATTN_SCALE
        p = jax.nn.softmax(s, axis=-1)
        a = p[:, :, 0] - lam * p[:, :, 1]
        return jnp.einsum('bhqk,bkhe->bqhe', a.astype(v.dtype), v)
    return _merge_blocks(lax.map(block, _query_blocks(q)))


def gqa_attend(q, k, v):
    def block(qb):
        s = jnp.einsum('bqhgd,bkhd->bhgqk', qb, k).astype(F32) * ATTN_SCALE
        p = jax.nn.softmax(s, axis=-1).astype(v.dtype)
        return jnp.einsum('bhgqk,bkhd->bqhgd', p, v)
    return _merge_blocks(lax.map(block, _query_blocks(q)))


def attn_qkv(u, w_qkv, qn_a, kn_a, qn_b, kn_b):
    b, l, _ = u.shape
    aq, ak, av, bq, bk, bv = jnp.split(u @ w_qkv, QKV_SPLITS, axis=-1)
    aq = rms_norm(aq.reshape(b, l, 2 * H_A, HEAD_DIM), qn_a)
    ak = rms_norm(ak.reshape(b, l, 2 * H_A, HEAD_DIM), kn_a)
    bq = rms_norm(bq.reshape(b, l, H_B, HEAD_DIM), qn_b)
    bk = rms_norm(bk.reshape(b, l, KV_B, HEAD_DIM), kn_b)
    return aq, ak, av.reshape(b, l, H_A, 2 * HEAD_DIM), bq, bk, bv.reshape(b, l, KV_B, HEAD_DIM)


def attn_out(aq, ak, av, bq, bk, bv, lam, lam_init, subln, w_o):
    b, l = aq.shape[:2]
    s = ak.shape[1]
    oa = diff_attend(aq.reshape(b, l, H_A, 2, HEAD_DIM), ak.reshape(b, s, H_A, 2, HEAD_DIM), av, lam)
    oa = rms_norm(oa, subln) * (1.0 - lam_init)
    ob = gqa_attend(bq.reshape(b, l, KV_B, G_B, HEAD_DIM), bk, bv)
    o = jnp.concatenate([oa.reshape(b, l, -1), ob.reshape(b, l, -1)], axis=-1)
    return o @ w_o


def hyena_filters(n_tok, w1, b1, fr1, w2, b2, fr2, w3):
    pos = jnp.arange(n_tok, dtype=F32)[:, None]
    t = jnp.linspace(0.0, 1.0, n_tok, dtype=F32)[:, None]
    bands = jnp.linspace(1e-4, HY_BANDS - 1, HY_BANDS, dtype=F32)[None, :]
    ang = (2.0 * math.pi / n_tok) * pos * bands
    z = jnp.concatenate([t, jnp.cos(ang), -jnp.sin(ang)], axis=-1)
    h = jnp.sin(fr1.astype(F32) * (z @ w1.astype(F32) + b1.astype(F32)))
    h = jnp.sin(fr2.astype(F32) * (h @ w2.astype(F32) + b2.astype(F32)))
    h = (h @ w3.astype(F32)).reshape(n_tok, HY_ORDER, 2, D_MODEL)
    deltas = jnp.abs(jnp.linspace(HY_MIN_DECAY, HY_MAX_DECAY, D_MODEL, dtype=F32))
    window = jnp.exp(-t * deltas[None, :])
    return h * window[:, None, None, :]


def hyena_mixer(u, w_in, b_in, conv_w, conv_b, f_w1, f_b1, f_fr1, f_w2, f_b2, f_fr2, f_w3, skip, w_out, b_out):
    b, l, _ = u.shape
    p = u @ w_in + b_in
    half = HY_SHORT // 2
    pp = jnp.pad(p, ((0, 0), (half, half), (0, 0)))
    p = sum(pp[:, j:j + l] * conv_w[j] for j in range(HY_SHORT)) + conv_b
    v, x1, x2 = jnp.split(p, 3, axis=-1)
    h = hyena_filters(l, f_w1, f_b1, f_fr1, f_w2, f_b2, f_fr2, f_w3)
    filt = jnp.concatenate([h[:, :, 0], jnp.zeros((1, HY_ORDER, D_MODEL), F32), h[:0:-1, :, 1]], axis=0)
    filt_f = jnp.fft.rfft(filt, axis=0)
    z = v
    for o, gate in enumerate((x1, x2)):
        zf32 = z.astype(F32)
        conv = jnp.fft.irfft(jnp.fft.rfft(zf32, n=2 * l, axis=1) * filt_f[None, :, o], n=2 * l, axis=1)[:, :l]
        z = gate * (conv + zf32 * skip[o].astype(F32)).astype(u.dtype)
    return z @ w_out + b_out


def moe_ffn(h, router_w, router_b, wg, wu, wd):
    b, l, d = h.shape
    t = h.reshape(-1, d)
    n = t.shape[0]
    probs = jax.nn.softmax((t @ router_w).astype(F32), axis=-1)
    sel = (probs + router_b.astype(F32)).reshape(n, N_GROUPS, EXPERTS_PER_GROUP)
    grp_score = jnp.sum(lax.top_k(sel, 2)[0], axis=-1)
    g = jnp.argmax(grp_score, axis=-1)
    in_grp = sel[jnp.arange(n), g]
    loc = lax.top_k(in_grp, TOP_K)[1]
    e_idx = g[:, None] * EXPERTS_PER_GROUP + loc
    w = jnp.take_along_axis(probs, e_idx, axis=-1)
    w = w / jnp.sum(w, axis=-1, keepdims=True)
    gates = jnp.sum(jax.nn.one_hot(e_idx, N_EXPERTS, dtype=F32) * w[..., None], axis=1).astype(t.dtype)
    out = jnp.zeros_like(t)
    for e in range(N_EXPERTS):
        y = (jax.nn.silu(t @ wg[e]) * (t @ wu[e])) @ wd[e]
        out = out + gates[:, e:e + 1] * y
    return out.reshape(b, l, d)


def setup_inputs(seed: int = 0) -> dict:
    key = jax.random.key(seed)
    ks = iter(jax.random.split(key, 64))
    nrm = lambda shape, scale: scale * jax.random.normal(next(ks), shape, F32)
    gain = lambda shape: 1.0 + 0.1 * jax.random.normal(next(ks), shape, F32)
    D = D_MODEL
    return {
        'x_prompt': nrm((BATCH, SEQ, D), 1.0),
        'x_sample': nrm((DEC_BATCH, DEC_SEQ, D), 1.0),
        'cache_a_k': nrm((DEC_BATCH, N_ATT, PAST_LEN, 2 * H_A, HEAD_DIM), 1.0),
        'cache_a_v': nrm((DEC_BATCH, N_ATT, PAST_LEN, H_A, 2 * HEAD_DIM), 1.0),
        'cache_b_k': nrm((DEC_BATCH, N_ATT, PAST_LEN, KV_B, HEAD_DIM), 1.0),
        'cache_b_v': nrm((DEC_BATCH, N_ATT, PAST_LEN, KV_B, HEAD_DIM), 1.0),
        'c': nrm((DEC_BATCH, D), 1.0),
        'c_ctx': nrm((D,), 1.0),
        'router_w': nrm((D, N_EXPERTS), D ** -0.5),
        'router_b': nrm((N_EXPERTS,), 0.01),
        'ada_w': nrm((DEPTH, D, 6 * D), 0.5 * D ** -0.5),
        'ada_b': nrm((DEPTH, 6 * D), 0.02),
        'norm1_g': gain((DEPTH, D)),
        'norm2_g': gain((DEPTH, D)),
        'w_qkv': nrm((N_ATT, D, QKV_WIDTH), D ** -0.5),
        'w_o': nrm((N_ATT, ATT_OUT, D), ATT_OUT ** -0.5),
        'qn_a': gain((N_ATT, HEAD_DIM)),
        'kn_a': gain((N_ATT, HEAD_DIM)),
        'qn_b': gain((N_ATT, HEAD_DIM)),
        'kn_b': gain((N_ATT, HEAD_DIM)),
        'lam_q1': nrm((N_ATT, HEAD_DIM), 0.1),
        'lam_k1': nrm((N_ATT, HEAD_DIM), 0.1),
        'lam_q2': nrm((N_ATT, HEAD_DIM), 0.1),
        'lam_k2': nrm((N_ATT, HEAD_DIM), 0.1),
        'subln_a': gain((N_ATT, 2 * HEAD_DIM)),
        'hy_w_in': nrm((N_HY, D, 3 * D), D ** -0.5),
        'hy_b_in': nrm((N_HY, 3 * D), 0.02),
        'hy_conv_w': nrm((N_HY, HY_SHORT, 3 * D), HY_SHORT ** -0.5),
        'hy_conv_b': nrm((N_HY, 3 * D), 0.02),
        'hy_f_w1': nrm((N_HY, HY_EMB, HY_FILTER_W), HY_EMB ** -0.5),
        'hy_f_b1': nrm((N_HY, HY_FILTER_W), 0.02),
        'hy_f_freq1': gain((N_HY, HY_FILTER_W)),
        'hy_f_w2': nrm((N_HY, HY_FILTER_W, HY_FILTER_W), HY_FILTER_W ** -0.5),
        'hy_f_b2': nrm((N_HY, HY_FILTER_W), 0.02),
        'hy_f_freq2': gain((N_HY, HY_FILTER_W)),
        'hy_f_w3': nrm((N_HY, HY_FILTER_W, HY_ORDER * 2 * D), 0.05 * HY_FILTER_W ** -0.5),
        'hy_bias': nrm((N_HY, HY_ORDER, D), 0.1),
        'hy_w_out': nrm((N_HY, D, D), D ** -0.5),
        'hy_b_out': nrm((N_HY, D), 0.02),
        'moe_wg': nrm((DEPTH, N_EXPERTS, D, D_FF_EXPERT), D ** -0.5),
        'moe_wu': nrm((DEPTH, N_EXPERTS, D, D_FF_EXPERT), D ** -0.5),
        'moe_wd': nrm((DEPTH, N_EXPERTS, D_FF_EXPERT, D), D_FF_EXPERT ** -0.5),
    }


def reference(x_prompt, x_sample, cache_a_k, cache_a_v, cache_b_k, cache_b_v, c, c_ctx,
              router_w, router_b, ada_w, ada_b, norm1_g, norm2_g, w_qkv, w_o,
              qn_a, kn_a, qn_b, kn_b, lam_q1, lam_k1, lam_q2, lam_k2, subln_a,
              hy_w_in, hy_b_in, hy_conv_w, hy_conv_b, hy_f_w1, hy_f_b1, hy_f_freq1,
              hy_f_w2, hy_f_b2, hy_f_freq2, hy_f_w3, hy_bias, hy_w_out, hy_b_out,
              moe_wg, moe_wu, moe_wd):
    cos, sin = axial_rope_tables(x_sample.shape[1])
    xp, xs = x_prompt, x_sample
    new_ak, new_av, new_bk, new_bv = [], [], [], []
    for l in range(DEPTH):
        mp = ada_modulation(c_ctx[None, :], ada_w[l], ada_b[l])
        ms = ada_modulation(c, ada_w[l], ada_b[l])
        up = modulate(rms_norm(xp, norm1_g[l]), mp[:, None, 0], mp[:, None, 1])
        us = modulate(rms_norm(xs, norm1_g[l]), ms[:, None, 0], ms[:, None, 1])
        if l % 2 == 0:
            i = l // 2
            lam_init = 0.8 - 0.6 * math.exp(-0.3 * l)
            lam = (jnp.exp(jnp.sum(lam_q1[i].astype(F32) * lam_k1[i].astype(F32)))
                   - jnp.exp(jnp.sum(lam_q2[i].astype(F32) * lam_k2[i].astype(F32))) + lam_init)
            aq, ak, av, bq, bk, bv = attn_qkv(up, w_qkv[i], qn_a[i], kn_a[i], qn_b[i], kn_b[i])
            new_ak.append(ak)
            new_av.append(av)
            new_bk.append(bk)
            new_bv.append(bv)
            op = attn_out(aq, ak, av, bq, bk, bv, lam, lam_init, subln_a[i], w_o[i])
            aq, ak, av, bq, bk, bv = attn_qkv(us, w_qkv[i], qn_a[i], kn_a[i], qn_b[i], kn_b[i])
            aq, ak, bq, bk = [apply_rope(t, cos, sin) for t in (aq, ak, bq, bk)]
            ak = jnp.concatenate([cache_a_k[:, i], ak], axis=1)
            av = jnp.concatenate([cache_a_v[:, i], av], axis=1)
            bk = jnp.concatenate([cache_b_k[:, i], bk], axis=1)
            bv = jnp.concatenate([cache_b_v[:, i], bv], axis=1)
            os_ = attn_out(aq, ak, av, bq, bk, bv, lam, lam_init, subln_a[i], w_o[i])
        else:
            j = l // 2
            hy = (hy_w_in[j], hy_b_in[j], hy_conv_w[j], hy_conv_b[j], hy_f_w1[j], hy_f_b1[j], hy_f_freq1[j],
                  hy_f_w2[j], hy_f_b2[j], hy_f_freq2[j], hy_f_w3[j], hy_bias[j], hy_w_out[j], hy_b_out[j])
            op = hyena_mixer(up, *hy)
            os_ = hyena_mixer(us, *hy)
        xp = xp + mp[:, None, 2] * op
        xs = xs + ms[:, None, 2] * os_
        hp = modulate(rms_norm(xp, norm2_g[l]), mp[:, None, 3], mp[:, None, 4])
        hs = modulate(rms_norm(xs, norm2_g[l]), ms[:, None, 3], ms[:, None, 4])
        xp = xp + mp[:, None, 5] * moe_ffn(hp, router_w, router_b, moe_wg[l], moe_wu[l], moe_wd[l])
        xs = xs + ms[:, None, 5] * moe_ffn(hs, router_w, router_b, moe_wg[l], moe_wu[l], moe_wd[l])
    return (xp, xs, jnp.stack(new_ak, axis=1), jnp.stack(new_av, axis=1), jnp.stack(new_bk, axis=1), jnp.stack(new_bv, axis=1))
```

```python
import functools
import math

import numpy as np
import jax
import jax.numpy as jnp
from jax import lax
from jax.experimental import pallas as pl
from jax.experimental.pallas import tpu as pltpu

F32 = jnp.float32
BF16 = jnp.bfloat16
I32 = jnp.int32

HEAD_DIM = 64
GRID_W = 64
ROPE_THETA = 10000.0
ROPE_PAIRS = HEAD_DIM // 4
N_GROUPS = 4
NORM_EPS = 1e-6
HY_MIN_DECAY = math.log(1e-2) / 1.5
HY_MAX_DECAY = math.log(1e-2) / 0.3

LANES = 128
MOD_ROWS = 8
VMEM_LIMIT_BYTES = 50 * 1024 * 1024

HIGHEST = lax.Precision.HIGHEST
NT_DIMS = (((1,), (1,)), ((), ()))


def _cparams(n_axes):
    return pltpu.CompilerParams(dimension_semantics=("arbitrary",) * n_axes,
                                vmem_limit_bytes=VMEM_LIMIT_BYTES)


def _sigmoid(x):
    return 1.0 / (1.0 + jnp.exp(-x))


def _norm_mod(x, g, mod_ref, shift_row):
    ms = jnp.mean(x * x, axis=-1, keepdims=True)
    y = x * lax.rsqrt(ms + NORM_EPS) * g
    shift = mod_ref[0, shift_row:shift_row + 1, :]
    scale = mod_ref[0, shift_row + 1:shift_row + 2, :]
    return y * (1.0 + scale) + shift


def _norm_mod_kernel(seg_ref, x_ref, g_ref, mod_ref, u_ref, *, shift_row):
    del seg_ref
    u_ref[...] = _norm_mod(x_ref[...], g_ref[...], mod_ref, shift_row).astype(u_ref.dtype)


def norm_mod(x, g, mod, seg, *, shift_row, tm):
    n, d = x.shape
    grid_spec = pltpu.PrefetchScalarGridSpec(
        num_scalar_prefetch=1, grid=(n // tm,),
        in_specs=[pl.BlockSpec((tm, d), lambda i, s: (i, 0)),
                  pl.BlockSpec((1, d), lambda i, s: (0, 0)),
                  pl.BlockSpec((1, MOD_ROWS, d), lambda i, s: (s[i], 0, 0))],
        out_specs=pl.BlockSpec((tm, d), lambda i, s: (i, 0)))
    return pl.pallas_call(
        functools.partial(_norm_mod_kernel, shift_row=shift_row),
        grid_spec=grid_spec, out_shape=jax.ShapeDtypeStruct((n, d), BF16),
        compiler_params=_cparams(1), name="norm_mod")(seg, x, g, mod)


def _first_index(vals, m):
    idx = jnp.full(vals[0].shape, len(vals) - 1, I32)
    for k in reversed(range(len(vals) - 1)):
        idx = jnp.where(vals[k] >= m, k, idx)
    return idx


def _pick(vals, idx):
    out = vals[-1]
    for k in reversed(range(len(vals) - 1)):
        out = jnp.where(idx == k, vals[k], out)
    return out


def _max_list(vals):
    m = vals[0]
    for v in vals[1:]:
        m = jnp.maximum(m, v)
    return m


def _moe_pre_kernel(seg_ref, x_ref, g_ref, mod_ref, rwt_ref, rb_ref, u_ref, e_ref, w_ref,
                    *, shift_row, n_groups):
    del seg_ref
    u = _norm_mod(x_ref[...], g_ref[...], mod_ref, shift_row)
    u_ref[...] = u
    logits = lax.dot_general(rwt_ref[...], u, NT_DIMS, precision=HIGHEST,
                             preferred_element_type=F32)
    n_exp, tm = logits.shape
    epg = n_exp // n_groups
    ex = jnp.exp(logits - jnp.max(logits, axis=0, keepdims=True))
    probs = ex / jnp.sum(ex, axis=0, keepdims=True)
    sel = probs + rb_ref[...]
    sel_rows = [sel[e:e + 1, :] for e in range(n_exp)]
    p_rows = [probs[e:e + 1, :] for e in range(n_exp)]
    scores, i1s, i2s, p1s, p2s = [], [], [], [], []
    for gi in range(n_groups):
        v = sel_rows[gi * epg:(gi + 1) * epg]
        p = p_rows[gi * epg:(gi + 1) * epg]
        m1 = _max_list(v)
        i1 = _first_index(v, m1)
        v2 = [jnp.where(i1 == k, -jnp.inf, v[k]) for k in range(epg)]
        m2 = _max_list(v2)
        i2 = _first_index(v2, m2)
        scores.append(m1 + m2)
        i1s.append(i1)
        i2s.append(i2)
        p1s.append(_pick(p, i1))
        p2s.append(_pick(p, i2))
    gstar = _first_index(scores, _max_list(scores))
    e1 = gstar * epg + _pick(i1s, gstar)
    e2 = gstar * epg + _pick(i2s, gstar)
    p1 = _pick(p1s, gstar)
    p2 = _pick(p2s, gstar)
    tot = p1 + p2
    e_ref[0:1, :] = e1
    e_ref[1:2, :] = e2
    e_ref[2:, :] = jnp.zeros((MOD_ROWS - 2, tm), I32)
    w_ref[0:1, :] = p1 / tot
    w_ref[1:2, :] = p2 / tot
    w_ref[2:, :] = jnp.zeros((MOD_ROWS - 2, tm), F32)


def moe_pre(x, g, mod, seg, router_wt, router_b_col, *, shift_row, tm):
    n, d = x.shape
    n_exp = router_wt.shape[0]
    grid_spec = pltpu.PrefetchScalarGridSpec(
        num_scalar_prefetch=1, grid=(n // tm,),
        in_specs=[pl.BlockSpec((tm, d), lambda i, s: (i, 0)),
                  pl.BlockSpec((1, d), lambda i, s: (0, 0)),
                  pl.BlockSpec((1, MOD_ROWS, d), lambda i, s: (s[i], 0, 0)),
                  pl.BlockSpec((n_exp, d), lambda i, s: (0, 0)),
                  pl.BlockSpec((n_exp, 1), lambda i, s: (0, 0))],
        out_specs=[pl.BlockSpec((tm, d), lambda i, s: (i, 0)),
                   pl.BlockSpec((MOD_ROWS, tm), lambda i, s: (0, i)),
                   pl.BlockSpec((MOD_ROWS, tm), lambda i, s: (0, i))])
    return pl.pallas_call(
        functools.partial(_moe_pre_kernel, shift_row=shift_row, n_groups=N_GROUPS),
        grid_spec=grid_spec,
        out_shape=[jax.ShapeDtypeStruct((n, d), F32),
                   jax.ShapeDtypeStruct((MOD_ROWS, n), I32),
                   jax.ShapeDtypeStruct((MOD_ROWS, n), F32)],
        compiler_params=_cparams(1), name="moe_pre")(seg, x, g, mod, router_wt, router_b_col)


def _linear_kernel(*refs, silu, has_bias, gate_row):
    refs = list(refs)
    refs.pop(0)
    a_ref, w_ref = refs.pop(0), refs.pop(0)
    bias_ref = refs.pop(0) if has_bias else None
    x_ref, mod_ref = (refs.pop(0), refs.pop(0)) if gate_row is not None else (None, None)
    o_ref, wb_ref = refs

    @pl.when(pl.program_id(1) == 0)
    def _():
        wb_ref[...] = w_ref[...].astype(BF16)

    a = a_ref[...]
    if silu:
        a = a * _sigmoid(a)
    acc = jnp.dot(a.astype(BF16), wb_ref[...], preferred_element_type=F32)
    if has_bias:
        acc = acc + bias_ref[...]
    if gate_row is not None:
        acc = x_ref[...] + mod_ref[0, gate_row:gate_row + 1, :] * acc
    o_ref[...] = acc.astype(o_ref.dtype)


def linear(a, w, seg, *, bias=None, resid=None, mod=None, gate_row=None, silu=False, tm, tn):
    m, k = a.shape
    n = w.shape[1]
    in_specs = [pl.BlockSpec((tm, k), lambda j, i, s: (i, 0)),
                pl.BlockSpec((k, tn), lambda j, i, s: (0, j))]
    args = [a, w]
    if bias is not None:
        in_specs.append(pl.BlockSpec((1, tn), lambda j, i, s: (0, j)))
        args.append(bias)
    if gate_row is not None:
        in_specs.append(pl.BlockSpec((tm, tn), lambda j, i, s: (i, j)))
        in_specs.append(pl.BlockSpec((1, MOD_ROWS, tn), lambda j, i, s: (s[i], 0, j)))
        args += [resid, mod]
    grid_spec = pltpu.PrefetchScalarGridSpec(
        num_scalar_prefetch=1, grid=(n // tn, m // tm), in_specs=in_specs,
        out_specs=pl.BlockSpec((tm, tn), lambda j, i, s: (i, j)),
        scratch_shapes=[pltpu.VMEM((k, tn), BF16)])
    return pl.pallas_call(
        functools.partial(_linear_kernel, silu=silu, has_bias=bias is not None, gate_row=gate_row),
        grid_spec=grid_spec, out_shape=jax.ShapeDtypeStruct((m, n), F32),
        compiler_params=_cparams(2), name="linear")(seg, *args)


def _qkv_kernel(*refs, rope, keep_f32, n_a, n_bq, scale):
    refs = list(refs)
    a_ref, w_ref, bd_ref, gains_ref = refs[:4]
    refs = refs[4:]
    if rope:
        c_ref, s_ref = refs[:2]
        refs = refs[2:]
    q_ref, k_ref, v_ref = refs[:3]
    refs = refs[3:]
    if keep_f32:
        kf_ref, vf_ref = refs[:2]
        refs = refs[2:]
    wb_ref, = refs

    @pl.when(pl.program_id(0) == 0)
    def _():
        wb_ref[...] = w_ref[...].astype(BF16)

    acc = jnp.dot(a_ref[...], wb_ref[...], preferred_element_type=F32)
    tm = acc.shape[0]
    bd = bd_ref[...]
    lane = lax.broadcasted_iota(I32, (tm, LANES), 1)
    low_half = lane < HEAD_DIM

    def headnorm(x, row):
        ms = jnp.dot((x * x).astype(BF16), bd, preferred_element_type=F32)
        return x * lax.rsqrt(ms + NORM_EPS) * gains_ref[row:row + 1, :]

    def rot(x):
        if not rope:
            return x
        partner = jnp.where((lane & ROPE_PAIRS) == 0,
                            pltpu.roll(x, LANES - ROPE_PAIRS, 1), pltpu.roll(x, ROPE_PAIRS, 1))
        return x * c_ref[...] + partner * s_ref[...]

    def chunk(col):
        return acc[:, col * LANES:(col + 1) * LANES]

    def dup(x):
        swapped = pltpu.roll(x, HEAD_DIM, 1)
        return jnp.where(low_half, x, swapped), jnp.where(low_half, swapped, x)

    for c in range(n_a):
        q_ref[:, c * LANES:(c + 1) * LANES] = (rot(headnorm(chunk(c), 0)) * scale).astype(BF16)
        kn = headnorm(chunk(n_a + c), 1)
        k_ref[:, c * LANES:(c + 1) * LANES] = rot(kn).astype(BF16)
        vv = chunk(2 * n_a + c)
        v_ref[:, c * LANES:(c + 1) * LANES] = vv.astype(BF16)
        if keep_f32:
            kf_ref[:, c * LANES:(c + 1) * LANES] = kn
            vf_ref[:, c * LANES:(c + 1) * LANES] = vv
    for c in range(n_bq):
        q_ref[:, (n_a + c) * LANES:(n_a + c + 1) * LANES] = (
            rot(headnorm(chunk(3 * n_a + c), 2)) * scale).astype(BF16)
    kb = headnorm(chunk(3 * n_a + n_bq), 3)
    vb = chunk(3 * n_a + n_bq + 1)
    k0, k1 = dup(rot(kb))
    v0, v1 = dup(vb)
    k_ref[:, n_a * LANES:(n_a + 1) * LANES] = k0.astype(BF16)
    k_ref[:, (n_a + 1) * LANES:(n_a + 2) * LANES] = k1.astype(BF16)
    v_ref[:, n_a * LANES:(n_a + 1) * LANES] = v0.astype(BF16)
    v_ref[:, (n_a + 1) * LANES:(n_a + 2) * LANES] = v1.astype(BF16)
    if keep_f32:
        kf_ref[:, n_a * LANES:(n_a + 1) * LANES] = kb
        vf_ref[:, n_a * LANES:(n_a + 1) * LANES] = vb


def qkv_project(u, w, bd, gains, rope_cs, *, keep_f32, n_a, n_bq, tm):
    m, d = u.shape
    n = w.shape[1]
    assert n == (3 * n_a + n_bq + 2) * LANES
    rope = rope_cs is not None
    in_specs = [pl.BlockSpec((tm, d), lambda i: (i, 0)),
                pl.BlockSpec((d, n), lambda i: (0, 0)),
                pl.BlockSpec((LANES, LANES), lambda i: (0, 0)),
                pl.BlockSpec((MOD_ROWS, LANES), lambda i: (0, 0))]
    args = [u, w, bd, gains]
    if rope:
        rows = rope_cs[0].shape[0] // tm
        in_specs += [pl.BlockSpec((tm, LANES), lambda i: (i % rows, 0))] * 2
        args += list(rope_cs)
    qw, kw = (n_a + n_bq) * LANES, (n_a + 2) * LANES
    out_specs = [pl.BlockSpec((tm, qw), lambda i: (i, 0)),
                 pl.BlockSpec((tm, kw), lambda i: (i, 0)),
                 pl.BlockSpec((tm, kw), lambda i: (i, 0))]
    out_shape = [jax.ShapeDtypeStruct((m, qw), BF16), jax.ShapeDtypeStruct((m, kw), BF16),
                 jax.ShapeDtypeStruct((m, kw), BF16)]
    if keep_f32:
        fw = (n_a + 1) * LANES
        out_specs += [pl.BlockSpec((tm, fw), lambda i: (i, 0))] * 2
        out_shape += [jax.ShapeDtypeStruct((m, fw), F32)] * 2
    return pl.pallas_call(
        functools.partial(_qkv_kernel, rope=rope, keep_f32=keep_f32, n_a=n_a, n_bq=n_bq,
                          scale=HEAD_DIM ** -0.5),
        grid=(m // tm,), in_specs=in_specs, out_specs=out_specs, out_shape=out_shape,
        scratch_shapes=[pltpu.VMEM((d, n), BF16)],
        compiler_params=_cparams(1), name="qkv")(*args)


def _attn_kernel(q_ref, k_ref, v_ref, lam_ref, sg_ref, o_ref,
                 q0_ref, q1_ref, m0_ref, l0_ref, a0_ref, m1_ref, l1_ref, a1_ref,
                 *, n_diff, out_scale):
    c = pl.program_id(1)
    j = pl.program_id(3)
    tq = q_ref.shape[0]
    lane = lax.broadcasted_iota(I32, (tq, LANES), 1)
    low_half = lane < HEAD_DIM

    @pl.when(j == 0)
    def _():
        qf = q_ref[...].astype(F32)
        q0_ref[...] = jnp.where(low_half, qf, 0.0).astype(BF16)
        q1_ref[...] = jnp.where(low_half, 0.0, qf).astype(BF16)
        for m_ref, l_ref, a_ref in ((m0_ref, l0_ref, a0_ref), (m1_ref, l1_ref, a1_ref)):
            m_ref[...] = jnp.full(m_ref.shape, -jnp.inf, F32)
            l_ref[...] = jnp.zeros(l_ref.shape, F32)
            a_ref[...] = jnp.zeros(a_ref.shape, F32)

    k = k_ref[...]
    v = v_ref[...]
    for qh_ref, m_ref, l_ref, a_ref in ((q0_ref, m0_ref, l0_ref, a0_ref),
                                        (q1_ref, m1_ref, l1_ref, a1_ref)):
        s = lax.dot_general(qh_ref[...], k, NT_DIMS, preferred_element_type=F32)
        m_prev = m_ref[...]
        m_new = jnp.maximum(m_prev, jnp.max(s, axis=-1, keepdims=True))
        alpha = jnp.exp(m_prev - m_new)
        p = jnp.exp(s - m_new)
        l_ref[...] = alpha * l_ref[...] + jnp.sum(p, axis=-1, keepdims=True)
        a_ref[...] = alpha * a_ref[...] + jnp.dot(p.astype(BF16), v, preferred_element_type=F32)
        m_ref[...] = m_new

    @pl.when(j == pl.num_programs(3) - 1)
    def _():
        o0 = a0_ref[...] / l0_ref[...]
        o1 = a1_ref[...] / l1_ref[...]
        od = o0 - lam_ref[...] * o1
        ms = jnp.mean(od * od, axis=-1, keepdims=True)
        od = od * lax.rsqrt(ms + NORM_EPS) * sg_ref[...] * out_scale
        og = jnp.where(low_half, o0, o1)
        o_ref[...] = jnp.where(c < n_diff, od, og).astype(o_ref.dtype)


def attention(q, k, v, lam_row, subln_row, *, n_diff, out_scale, tq, tk):
    b, lq, qw = q.shape
    s = k.shape[1]
    n_chunks = qw // LANES

    def kv_chunk(c):
        return jnp.where(c < n_diff, c, n_diff + (c - n_diff) // 2)

    return pl.pallas_call(
        functools.partial(_attn_kernel, n_diff=n_diff, out_scale=out_scale),
        grid=(b, n_chunks, lq // tq, s // tk),
        in_specs=[pl.BlockSpec((None, tq, LANES), lambda bi, c, i, j: (bi, i, c)),
                  pl.BlockSpec((None, tk, LANES), lambda bi, c, i, j: (bi, j, kv_chunk(c))),
                  pl.BlockSpec((None, tk, LANES), lambda bi, c, i, j: (bi, j, kv_chunk(c))),
                  pl.BlockSpec((1, LANES), lambda bi, c, i, j: (0, 0)),
                  pl.BlockSpec((1, LANES), lambda bi, c, i, j: (0, 0))],
        out_specs=pl.BlockSpec((None, tq, LANES), lambda bi, c, i, j: (bi, i, c)),
        out_shape=jax.ShapeDtypeStruct((b, lq, qw), BF16),
        scratch_shapes=[pltpu.VMEM((tq, LANES), BF16), pltpu.VMEM((tq, LANES), BF16),
                        pltpu.VMEM((tq, 1), F32), pltpu.VMEM((tq, 1), F32),
                        pltpu.VMEM((tq, LANES), F32),
                        pltpu.VMEM((tq, 1), F32), pltpu.VMEM((tq, 1), F32),
                        pltpu.VMEM((tq, LANES), F32)],
        compiler_params=_cparams(4), name="attention")(q, k, v, lam_row, subln_row)


def _row_copy(src_hbm, row, dst_ref, dst_row, sem):
    return pltpu.make_async_copy(src_hbm.at[pl.ds(row, 1), :], dst_ref.at[pl.ds(dst_row, 1), :], sem)


def _gather_kernel(idx_ref, src_hbm, o_ref, sem):
    rows = o_ref.shape[0]

    def start(r, carry):
        _row_copy(src_hbm, idx_ref[0, 0, r], o_ref, r, sem).start()
        return carry

    def wait(r, carry):
        _row_copy(src_hbm, 0, o_ref, r, sem).wait()
        return carry

    lax.fori_loop(0, rows, start, 0)
    lax.fori_loop(0, rows, wait, 0)


def gather_rows(src, idx, *, tg):
    p = idx.shape[0]
    d = src.shape[1]
    return pl.pallas_call(
        _gather_kernel, grid=(p // tg,),
        in_specs=[pl.BlockSpec((1, 1, tg), lambda i: (i, 0, 0), memory_space=pltpu.SMEM),
                  pl.BlockSpec(memory_space=pl.ANY)],
        out_specs=pl.BlockSpec((tg, d), lambda i: (i, 0)),
        out_shape=jax.ShapeDtypeStruct((p, d), src.dtype),
        scratch_shapes=[pltpu.SemaphoreType.DMA(())],
        compiler_params=_cparams(1), name="moe_gather")(idx.reshape(p // tg, 1, tg), src)


def _ffn_kernel(te_ref, tv_ref, x_ref, wg_ref, wu_ref, wd_ref, o_ref, wgb_ref, wub_ref, wdb_ref):
    t = pl.program_id(0)
    prev = te_ref[jnp.maximum(t - 1, 0)]
    new_expert = jnp.logical_or(t == 0, te_ref[t] != prev)

    @pl.when(new_expert)
    def _():
        wgb_ref[...] = wg_ref[...].astype(BF16)
        wub_ref[...] = wu_ref[...].astype(BF16)
        wdb_ref[...] = wd_ref[...].astype(BF16)

    @pl.when(tv_ref[t] != 0)
    def _():
        x = x_ref[...].astype(BF16)
        gate = jnp.dot(x, wgb_ref[...], preferred_element_type=F32)
        up = jnp.dot(x, wub_ref[...], preferred_element_type=F32)
        h = (gate * _sigmoid(gate) * up).astype(BF16)
        o_ref[...] = jnp.dot(h, wdb_ref[...], preferred_element_type=F32)

    @pl.when(tv_ref[t] == 0)
    def _():
        o_ref[...] = jnp.zeros(o_ref.shape, o_ref.dtype)


def moe_ffn(xs, tile_expert, tile_valid, wg, wu, wd, *, tm):
    p, d = xs.shape
    f = wg.shape[2]
    grid_spec = pltpu.PrefetchScalarGridSpec(
        num_scalar_prefetch=2, grid=(p // tm,),
        in_specs=[pl.BlockSpec((tm, d), lambda t, te, tv: (t, 0)),
                  pl.BlockSpec((None, d, f), lambda t, te, tv: (te[t], 0, 0)),
                  pl.BlockSpec((None, d, f), lambda t, te, tv: (te[t], 0, 0)),
                  pl.BlockSpec((None, f, d), lambda t, te, tv: (te[t], 0, 0))],
        out_specs=pl.BlockSpec((tm, d), lambda t, te, tv: (t, 0)),
        scratch_shapes=[pltpu.VMEM((d, f), BF16), pltpu.VMEM((d, f), BF16),
                        pltpu.VMEM((f, d), BF16)])
    return pl.pallas_call(
        _ffn_kernel, grid_spec=grid_spec, out_shape=jax.ShapeDtypeStruct((p, d), F32),
        compiler_params=_cparams(1), name="moe_ffn")(tile_expert, tile_valid, xs, wg, wu, wd)


def _combine_kernel(seg_ref, slot_ref, y_hbm, x_ref, w0_ref, w1_ref, mod_ref, o_ref, buf_ref, sem,
                    *, gate_row):
    del seg_ref
    rows = x_ref.shape[0]

    def start(r, carry):
        for kk in range(2):
            _row_copy(y_hbm, slot_ref[0, 0, 2 * r + kk], buf_ref.at[kk], r, sem).start()
        return carry

    def wait(r, carry):
        for kk in range(2):
            _row_copy(y_hbm, 0, buf_ref.at[kk], r, sem).wait()
        return carry

    lax.fori_loop(0, rows, start, 0)
    lax.fori_loop(0, rows, wait, 0)
    moe = w0_ref[...] * buf_ref[0] + w1_ref[...] * buf_ref[1]
    o_ref[...] = x_ref[...] + mod_ref[0, gate_row:gate_row + 1, :] * moe


def moe_combine(y, slots, x, w0, w1, mod, seg, *, gate_row, tc):
    n, d = x.shape
    grid_spec = pltpu.PrefetchScalarGridSpec(
        num_scalar_prefetch=1, grid=(n // tc,),
        in_specs=[pl.BlockSpec((1, 1, 2 * tc), lambda i, s: (i, 0, 0), memory_space=pltpu.SMEM),
                  pl.BlockSpec(memory_space=pl.ANY),
                  pl.BlockSpec((tc, d), lambda i, s: (i, 0)),
                  pl.BlockSpec((tc, 1), lambda i, s: (i, 0)),
                  pl.BlockSpec((tc, 1), lambda i, s: (i, 0)),
                  pl.BlockSpec((1, MOD_ROWS, d), lambda i, s: (s[i], 0, 0))],
        out_specs=pl.BlockSpec((tc, d), lambda i, s: (i, 0)),
        scratch_shapes=[pltpu.VMEM((2, tc, d), F32), pltpu.SemaphoreType.DMA(())])
    return pl.pallas_call(
        functools.partial(_combine_kernel, gate_row=gate_row),
        grid_spec=grid_spec, out_shape=jax.ShapeDtypeStruct((n, d), F32),
        compiler_params=_cparams(1), name="moe_combine")(
            seg, slots.reshape(n // tc, 1, 2 * tc), y, x, w0, w1, mod)


def _route(e_idx, *, n_exp, tm):
    n = e_idx.shape[0]
    e_flat = e_idx.reshape(-1)
    onehot = (e_flat[:, None] == jnp.arange(n_exp, dtype=I32)[None, :]).astype(I32)
    csum = jnp.cumsum(onehot, axis=0)
    rank = jnp.sum(onehot * csum, axis=1) - 1
    counts = csum[-1]
    padded = ((counts + tm - 1) // tm) * tm
    ends = jnp.cumsum(padded)
    starts = ends - padded
    slot = jnp.sum(onehot * starts[None, :], axis=1) + rank
    p_max = 2 * n + n_exp * tm
    tile_start = jnp.arange(p_max // tm, dtype=I32) * tm
    tile_expert = jnp.minimum(jnp.sum((tile_start[:, None] >= ends[None, :]).astype(I32), axis=1),
                              n_exp - 1)
    tile_valid = (tile_start < ends[-1]).astype(I32)
    token = jnp.arange(2 * n, dtype=I32) // 2
    src = jnp.zeros((p_max,), I32).at[slot].set(token)
    return slot.reshape(n, 2), src, tile_expert, tile_valid


def _linear_nt_kernel(wt_ref, u_ref, bias_ref, o_ref, wb_ref):
    @pl.when(jnp.logical_and(pl.program_id(1) == 0, pl.program_id(2) == 0))
    def _():
        wb_ref[...] = wt_ref[...].astype(BF16)

    acc = lax.dot_general(wb_ref[...], u_ref[...], NT_DIMS, preferred_element_type=F32)
    o_ref[...] = acc + bias_ref[...]


def linear_nt(wt, u, bias_col, *, tn, tl):
    n, k = wt.shape
    b, l, _ = u.shape
    return pl.pallas_call(
        _linear_nt_kernel, grid=(n // tn, b, l // tl),
        in_specs=[pl.BlockSpec((tn, k), lambda j, bi, i: (j, 0)),
                  pl.BlockSpec((None, tl, k), lambda j, bi, i: (bi, i, 0)),
                  pl.BlockSpec((tn, 1), lambda j, bi, i: (j, 0))],
        out_specs=pl.BlockSpec((None, tn, tl), lambda j, bi, i: (bi, j, i)),
        out_shape=jax.ShapeDtypeStruct((b, n, l), F32),
        scratch_shapes=[pltpu.VMEM((tn, k), BF16)],
        compiler_params=_cparams(3), name="linear_nt")(wt, u, bias_col)


def _filter_mlp_kernel(z_ref, w1_ref, b1_ref, f1_ref, w2_ref, b2_ref, f2_ref, o_ref):
    h = jnp.dot(w1_ref[...], z_ref[...], precision=HIGHEST, preferred_element_type=F32)
    h = jnp.sin(f1_ref[...] * (h + b1_ref[...]))
    h = jnp.dot(w2_ref[...], h, precision=HIGHEST, preferred_element_type=F32)
    o_ref[...] = jnp.sin(f2_ref[...] * (h + b2_ref[...]))


def filter_mlp(zt, w1t, b1, f1, w2t, b2, f2):
    width, l = w2t.shape[0], zt.shape[1]
    return pl.pallas_call(
        _filter_mlp_kernel, out_shape=jax.ShapeDtypeStruct((width, l), F32),
        compiler_params=pltpu.CompilerParams(vmem_limit_bytes=VMEM_LIMIT_BYTES),
        name="hyena_filter_mlp")(zt, w1t, b1, f1, w2t, b2, f2)


def _filter_out_kernel(w3_ref, h_ref, t_ref, delta_ref, o_ref):
    acc = jnp.dot(w3_ref[...].astype(BF16), h_ref[...].astype(BF16), preferred_element_type=F32)
    o_ref[...] = acc * jnp.exp(-t_ref[...] * delta_ref[...])


def filter_out(w3t, h2t, t_row, delta_col, *, tr):
    rows, width = w3t.shape
    l = h2t.shape[1]
    d_tiles = delta_col.shape[0] // tr
    return pl.pallas_call(
        _filter_out_kernel, grid=(rows // tr,),
        in_specs=[pl.BlockSpec((tr, width), lambda i: (i, 0)),
                  pl.BlockSpec((width, l), lambda i: (0, 0)),
                  pl.BlockSpec((1, l), lambda i: (0, 0)),
                  pl.BlockSpec((tr, 1), lambda i: (i % d_tiles, 0))],
        out_specs=pl.BlockSpec((tr, l), lambda i: (i, 0)),
        out_shape=jax.ShapeDtypeStruct((rows, l), F32),
        compiler_params=_cparams(1), name="hyena_filter_out")(w3t, h2t, t_row, delta_col)


def _spectrum_kernel(a_ref, b_ref, wf_ref, sgn_ref, o_ref):
    a = a_ref[...]
    lane = lax.broadcasted_iota(I32, a.shape, 1)
    a = jnp.where(lane == 0, 0.0, a)
    wf = wf_ref[...]
    ta = jnp.dot(a.astype(BF16), wf, preferred_element_type=F32)
    tb = jnp.dot(b_ref[...].astype(BF16), wf, preferred_element_type=F32)
    o_ref[...] = sgn_ref[...] * ta + tb


def filter_spectra(f_full, wf, sgn, *, p, tr):
    n_ord, d, two_l = f_full.shape
    nseg = two_l // p - 1
    return pl.pallas_call(
        _spectrum_kernel, grid=(n_ord, d // tr, nseg),
        in_specs=[pl.BlockSpec((None, tr, p), lambda o, i, j: (o, i, j)),
                  pl.BlockSpec((None, tr, p), lambda o, i, j: (o, i, j + 1)),
                  pl.BlockSpec((p, 2 * p), lambda o, i, j: (0, 0)),
                  pl.BlockSpec((1, 2 * p), lambda o, i, j: (0, 0))],
        out_specs=pl.BlockSpec((None, None, tr, 2 * p), lambda o, i, j: (o, j, i, 0)),
        out_shape=jax.ShapeDtypeStruct((n_ord, nseg, d, 2 * p), F32),
        compiler_params=_cparams(3), name="hyena_filter_spectra")(f_full, f_full, wf, sgn)


def _hyena_conv_kernel(pv_ref, p1_ref, p2_ref, cv_ref, c1_ref, c2_ref, sk_ref, g_ref,
                       wf_ref, wi_ref, o_ref, *, p, nb):
    td, l = pv_ref.shape
    lane_l = lax.broadcasted_iota(I32, (td, l), 1)
    lane_p = lax.broadcasted_iota(I32, (td, p), 1)
    dc = lane_p == 0

    def short_conv(x_ref, cw_ref):
        x = x_ref[...]
        cw = cw_ref[...]
        prev = jnp.where(lane_l == 0, 0.0, pltpu.roll(x, 1, 1))
        nxt = jnp.where(lane_l == l - 1, 0.0, pltpu.roll(x, l - 1, 1))
        return prev * cw[:, 0:1] + x * cw[:, 1:2] + nxt * cw[:, 2:3] + cw[:, 3:4]

    def long_conv(z, order):
        zb = jnp.concatenate([z[:, j * p:(j + 1) * p] for j in range(nb)], axis=0).astype(BF16)
        zh = jnp.dot(zb, wf_ref[...], preferred_element_type=F32)
        blocks = []
        for i in range(nb):
            t_rr = t_ii = t_ri = t_ir = None
            for j in range(nb):
                zr = zh[j * td:(j + 1) * td, :p]
                zi = zh[j * td:(j + 1) * td, p:]
                g = g_ref[order, i - j + nb - 1]
                gr, gi = g[:, :p], g[:, p:]
                if t_rr is None:
                    t_rr, t_ii, t_ri, t_ir = zr * gr, zi * gi, zr * gi, zi * gr
                else:
                    t_rr, t_ii = t_rr + zr * gr, t_ii + zi * gi
                    t_ri, t_ir = t_ri + zr * gi, t_ir + zi * gr
            yr = t_rr - jnp.where(dc, 0.0, t_ii)
            yi = jnp.where(dc, t_ii, t_ri + t_ir)
            blocks.append(jnp.concatenate([yr, yi], axis=1))
        yh = jnp.concatenate(blocks, axis=0).astype(BF16)
        y = jnp.dot(yh, wi_ref[...], preferred_element_type=F32)
        return jnp.concatenate([y[i * td:(i + 1) * td] for i in range(nb)], axis=1)

    v = short_conv(pv_ref, cv_ref)
    sk = sk_ref[...]
    z1 = short_conv(p1_ref, c1_ref) * (long_conv(v, 0) + v * sk[:, 0:1])
    o_ref[...] = short_conv(p2_ref, c2_ref) * (long_conv(z1, 1) + z1 * sk[:, 1:2])


def hyena_conv(pt, cwb, sk, g, wf, wi, *, p, td):
    b, d3, l = pt.shape
    d = d3 // 3
    nb = l // p
    n_ord, nseg = g.shape[:2]
    dt = d // td
    return pl.pallas_call(
        functools.partial(_hyena_conv_kernel, p=p, nb=nb),
        grid=(dt, b),
        in_specs=[pl.BlockSpec((None, td, l), lambda i, bi: (bi, i, 0)),
                  pl.BlockSpec((None, td, l), lambda i, bi: (bi, dt + i, 0)),
                  pl.BlockSpec((None, td, l), lambda i, bi: (bi, 2 * dt + i, 0)),
                  pl.BlockSpec((td, MOD_ROWS), lambda i, bi: (i, 0)),
                  pl.BlockSpec((td, MOD_ROWS), lambda i, bi: (dt + i, 0)),
                  pl.BlockSpec((td, MOD_ROWS), lambda i, bi: (2 * dt + i, 0)),
                  pl.BlockSpec((td, MOD_ROWS), lambda i, bi: (i, 0)),
                  pl.BlockSpec((n_ord, nseg, td, 2 * p), lambda i, bi: (0, 0, i, 0)),
                  pl.BlockSpec((p, 2 * p), lambda i, bi: (0, 0)),
                  pl.BlockSpec((2 * p, p), lambda i, bi: (0, 0))],
        out_specs=pl.BlockSpec((None, td, l), lambda i, bi: (bi, i, 0)),
        out_shape=jax.ShapeDtypeStruct((b, d, l), F32),
        compiler_params=_cparams(2), name="hyena_conv")(pt, pt, pt, cwb, cwb, cwb, sk, g, wf, wi)


def _dft_tables(p):
    n = jnp.arange(p, dtype=I32)
    ang = (jnp.pi / p) * ((n[:, None] * n[None, :]) % (2 * p)).astype(F32)
    cos, sin = jnp.cos(ang), jnp.sin(ang)
    alt = jnp.where(n % 2 == 0, 1.0, -1.0).astype(F32)
    is_dc = (n == 0)
    fwd_im = jnp.where(is_dc[None, :], alt[:, None], -sin)
    wf = jnp.concatenate([cos, fwd_im], axis=1)
    inv_re = jnp.where(is_dc[:, None], 0.5 / p, cos / p)
    inv_im = jnp.where(is_dc[:, None], (0.5 / p) * alt[None, :], -sin / p)
    wi = jnp.concatenate([inv_re, inv_im], axis=0)
    sgn = jnp.concatenate([alt, alt])[None, :]
    return wf.astype(BF16), wi.astype(BF16), sgn


def _hyena_filter_spectra(l, p, prm, wf, sgn, d):
    f_w1, f_b1, f_fr1, f_w2, f_b2, f_fr2, f_w3 = prm
    emb = f_w1.shape[0]
    bands_n = (emb - 1) // 2
    pos = jnp.arange(l, dtype=F32)[None, :]
    t = jnp.linspace(0.0, 1.0, l, dtype=F32)[None, :]
    bands = jnp.linspace(1e-4, bands_n - 1, bands_n, dtype=F32)[:, None]
    ang = (2.0 * math.pi / l) * pos * bands
    emb_pad = -(-emb // LANES) * LANES
    zt = jnp.concatenate([t, jnp.cos(ang), -jnp.sin(ang), jnp.zeros((emb_pad - emb, l), F32)], axis=0)
    w1t = jnp.pad(f_w1.T, ((0, 0), (0, emb_pad - emb)))
    h2t = filter_mlp(zt, w1t, f_b1[:, None], f_fr1[:, None], f_w2.T, f_b2[:, None], f_fr2[:, None])
    deltas = jnp.abs(jnp.linspace(HY_MIN_DECAY, HY_MAX_DECAY, d, dtype=F32))[:, None]
    ht = filter_out(f_w3.T, h2t, t, deltas, tr=256)
    n_ord = ht.shape[0] // (2 * d)
    ht = ht.reshape(n_ord, 2, d, l)
    f_full = jnp.concatenate([jnp.zeros((n_ord, d, 1), F32), jnp.flip(ht[:, 1, :, 1:], axis=-1),
                              ht[:, 0]], axis=-1)
    return filter_spectra(f_full, wf, sgn, p=p, tr=256)


def _rope_tables(n_tok):
    rows = n_tok // GRID_W
    row = jnp.repeat(jnp.arange(rows), GRID_W).astype(F32)
    col = jnp.tile(jnp.arange(GRID_W), rows).astype(F32)
    inv = ROPE_THETA ** (-jnp.arange(ROPE_PAIRS, dtype=F32) / ROPE_PAIRS)
    ang_r, ang_c = row[:, None] * inv, col[:, None] * inv
    cos_h = jnp.concatenate([jnp.cos(ang_r)] * 2 + [jnp.cos(ang_c)] * 2, axis=-1)
    sin_h = jnp.concatenate([-jnp.sin(ang_r), jnp.sin(ang_r), -jnp.sin(ang_c), jnp.sin(ang_c)],
                            axis=-1)
    reps = LANES // HEAD_DIM
    return jnp.tile(cos_h, (1, reps)), jnp.tile(sin_h, (1, reps))


def _dup_heads(x):
    h0, h1 = x[..., :HEAD_DIM], x[..., HEAD_DIM:]
    return jnp.concatenate([h0, h0, h1, h1], axis=-1)


def kernel(x_prompt, x_sample, cache_a_k, cache_a_v, cache_b_k, cache_b_v, c, c_ctx,
           router_w, router_b, ada_w, ada_b, norm1_g, norm2_g, w_qkv, w_o,
           qn_a, kn_a, qn_b, kn_b, lam_q1, lam_k1, lam_q2, lam_k2, subln_a,
           hy_w_in, hy_b_in, hy_conv_w, hy_conv_b, hy_f_w1, hy_f_b1, hy_f_freq1,
           hy_f_w2, hy_f_b2, hy_f_freq2, hy_f_w3, hy_bias, hy_w_out, hy_b_out,
           moe_wg, moe_wu, moe_wd):
    nb_p, seq, d = x_prompt.shape
    nb_s, dec_seq, _ = x_sample.shape
    past = cache_a_k.shape[2]
    depth = ada_w.shape[0]
    n_exp = router_w.shape[1]
    n_p, n_s = nb_p * seq, nb_s * dec_seq
    n_tok = n_p + n_s
    n_a = cache_a_v.shape[3]
    n_bq = (w_qkv.shape[2] // LANES) - 3 * n_a - 2
    assert cache_b_k.shape[3] * HEAD_DIM == LANES and n_bq == 2 * (cache_b_k.shape[3])

    tm = 512
    assert n_p % tm == 0 and dec_seq % tm == 0
    seg_rows = np.concatenate([np.zeros(n_p, np.int32),
                               1 + np.arange(n_s, dtype=np.int32) // dec_seq])
    seg = jnp.asarray(seg_rows[::tm])
    tc = 256
    seg_c = jnp.asarray(seg_rows[::tc])
    seg0 = jnp.zeros((1,), I32)

    x = jnp.concatenate([x_prompt.reshape(n_p, d), x_sample.reshape(n_s, d)], axis=0)
    cvec = jnp.concatenate([c_ctx[None, :], c, jnp.zeros((16 - 1 - nb_s, d), F32)], axis=0)
    bd = (jnp.arange(LANES)[:, None] // HEAD_DIM == jnp.arange(LANES)[None, :] // HEAD_DIM)
    bd = (bd.astype(F32) / HEAD_DIM).astype(BF16)
    rope_cs = _rope_tables(dec_seq)
    router_wt = router_w.T
    router_b_col = router_b[:, None]
    reps = LANES // HEAD_DIM

    new_ak, new_av, new_bk, new_bv = [], [], [], []
    for l in range(depth):
        m = linear(cvec, ada_w[l], seg0, bias=ada_b[l][None, :], silu=True, tm=16, tn=1024)
        mod = jnp.pad(m[:1 + nb_s].reshape(1 + nb_s, 6, d), ((0, 0), (0, MOD_ROWS - 6), (0, 0)))
        u = norm_mod(x, norm1_g[l][None, :], mod, seg, shift_row=0, tm=tm)
        if l % 2 == 0:
            i = l // 2
            lam_init = 0.8 - 0.6 * math.exp(-0.3 * l)
            lam = (jnp.exp(jnp.sum(lam_q1[i] * lam_k1[i])) - jnp.exp(jnp.sum(lam_q2[i] * lam_k2[i]))
                   + lam_init)
            lam_row = jnp.full((1, LANES), lam, F32)
            gains = jnp.concatenate(
                [jnp.tile(gn[i], reps)[None, :] for gn in (qn_a, kn_a, qn_b, kn_b)]
                + [jnp.zeros((MOD_ROWS - 4, LANES), F32)], axis=0)
            sub_row = subln_a[i][None, :]
            qp, kp, vp, kf, vf = qkv_project(u[:n_p], w_qkv[i], bd, gains, None,
                                             keep_f32=True, n_a=n_a, n_bq=n_bq, tm=256)
            aw = n_a * LANES
            new_ak.append(kf[:, :aw].reshape(nb_p, seq, 2 * n_a, HEAD_DIM))
            new_av.append(vf[:, :aw].reshape(nb_p, seq, n_a, 2 * HEAD_DIM))
            new_bk.append(kf[:, aw:].reshape(nb_p, seq, LANES // HEAD_DIM, HEAD_DIM))
            new_bv.append(vf[:, aw:].reshape(nb_p, seq, LANES // HEAD_DIM, HEAD_DIM))
            op = attention(qp.reshape(nb_p, seq, -1), kp.reshape(nb_p, seq, -1),
                           vp.reshape(nb_p, seq, -1), lam_row, sub_row,
                           n_diff=n_a, out_scale=1.0 - lam_init, tq=seq, tk=seq)
            qs, ks, vs = qkv_project(u[n_p:], w_qkv[i], bd, gains, rope_cs,
                                     keep_f32=False, n_a=n_a, n_bq=n_bq, tm=256)
            ck = jnp.concatenate([cache_a_k[:, i].reshape(nb_s, past, aw),
                                  _dup_heads(cache_b_k[:, i].reshape(nb_s, past, LANES))],
                                 axis=-1).astype(BF16)
            cv = jnp.concatenate([cache_a_v[:, i].reshape(nb_s, past, aw),
                                  _dup_heads(cache_b_v[:, i].reshape(nb_s, past, LANES))],
                                 axis=-1).astype(BF16)
            k_full = jnp.concatenate([ck, ks.reshape(nb_s, dec_seq, -1)], axis=1)
            v_full = jnp.concatenate([cv, vs.reshape(nb_s, dec_seq, -1)], axis=1)
            os_ = attention(qs.reshape(nb_s, dec_seq, -1), k_full, v_full, lam_row, sub_row,
                            n_diff=n_a, out_scale=1.0 - lam_init, tq=512, tk=512)
            o = jnp.concatenate([op.reshape(n_p, -1), os_.reshape(n_s, -1)], axis=0)
            x = linear(o, w_o[i], seg, resid=x, mod=mod, gate_row=2, tm=tm, tn=d)
        else:
            j = l // 2
            w_in_t = hy_w_in[j].T
            b_in_col = hy_b_in[j][:, None]
            cwb = jnp.concatenate([hy_conv_w[j].T, hy_conv_b[j][:, None],
                                   jnp.zeros((3 * d, MOD_ROWS - 4), F32)], axis=1)
            sk = jnp.concatenate([hy_bias[j].T, jnp.zeros((d, MOD_ROWS - 2), F32)], axis=1)
            fprm = (hy_f_w1[j], hy_f_b1[j], hy_f_freq1[j], hy_f_w2[j], hy_f_b2[j], hy_f_freq2[j],
                    hy_f_w3[j])
            zs = []
            for ub, nbat, ln, p, td, tl in ((u[:n_p], nb_p, seq, seq, 256, seq),
                                            (u[n_p:], nb_s, dec_seq, 512, 64, 512)):
                wf, wi, sgn = _dft_tables(p)
                g = _hyena_filter_spectra(ln, p, fprm, wf, sgn, d)
                pt = linear_nt(w_in_t, ub.reshape(nbat, ln, d), b_in_col, tn=512, tl=tl)
                zt = hyena_conv(pt, cwb, sk, g, wf, wi, p=p, td=td)
                zs.append(jnp.transpose(zt, (0, 2, 1)).reshape(nbat * ln, d).astype(BF16))
            z = jnp.concatenate(zs, axis=0)
            x = linear(z, hy_w_out[j], seg, bias=hy_b_out[j][None, :], resid=x, mod=mod,
                       gate_row=2, tm=tm, tn=d)
        h, e_t, w_t = moe_pre(x, norm2_g[l][None, :], mod, seg, router_wt, router_b_col,
                              shift_row=3, tm=tm)
        tmg = 256
        slots, src, tile_expert, tile_valid = _route(e_t[:2].T, n_exp=n_exp, tm=tmg)
        xs = gather_rows(h, src, tg=tmg)
        y = moe_ffn(xs, tile_expert, tile_valid, moe_wg[l], moe_wu[l], moe_wd[l], tm=tmg)
        x = moe_combine(y, slots, x, w_t[0][:, None], w_t[1][:, None], mod, seg_c, gate_row=5, tc=tc)

    y_prompt = x[:n_p].reshape(nb_p, seq, d)
    y_sample = x[n_p:].reshape(nb_s, dec_seq, d)
    return (y_prompt, y_sample, jnp.stack(new_ak, axis=1), jnp.stack(new_av, axis=1),
            jnp.stack(new_bk, axis=1), jnp.stack(new_bv, axis=1))
```

```python
import functools
import math

import numpy as np
import jax
import jax.numpy as jnp
from jax import lax
from jax.experimental import pallas as pl
from jax.experimental.pallas import tpu as pltpu

F32 = jnp.float32
BF16 = jnp.bfloat16
I32 = jnp.int32

HEAD_DIM = 64
GRID_W = 64
ROPE_THETA = 10000.0
ROPE_PAIRS = HEAD_DIM // 4
N_GROUPS = 4
NORM_EPS = 1e-6
HY_MIN_DECAY = math.log(1e-2) / 1.5
HY_MAX_DECAY = math.log(1e-2) / 0.3

LANES = 128
MOD_ROWS = 8
VMEM_LIMIT_BYTES = 50 * 1024 * 1024

HIGHEST = lax.Precision.HIGHEST
NT_DIMS = (((1,), (1,)), ((), ()))


def _cparams(n_axes):
    return pltpu.CompilerParams(dimension_semantics=("arbitrary",) * n_axes,
                                vmem_limit_bytes=VMEM_LIMIT_BYTES)


def _sigmoid(x):
    return 1.0 / (1.0 + jnp.exp(-x))


def _norm_mod(x, g, mod_ref, shift_row):
    ms = jnp.mean(x * x, axis=-1, keepdims=True)
    y = x * lax.rsqrt(ms + NORM_EPS) * g
    shift = mod_ref[0, shift_row:shift_row + 1, :]
    scale = mod_ref[0, shift_row + 1:shift_row + 2, :]
    return y * (1.0 + scale) + shift


def _norm_mod_kernel(seg_ref, x_ref, g_ref, mod_ref, u_ref, *, shift_row):
    del seg_ref
    u_ref[...] = _norm_mod(x_ref[...], g_ref[...], mod_ref, shift_row).astype(u_ref.dtype)


def norm_mod(x, g, mod, seg, *, shift_row, tm):
    n, d = x.shape
    grid_spec = pltpu.PrefetchScalarGridSpec(
        num_scalar_prefetch=1, grid=(n // tm,),
        in_specs=[pl.BlockSpec((tm, d), lambda i, s: (i, 0)),
                  pl.BlockSpec((1, d), lambda i, s: (0, 0)),
                  pl.BlockSpec((1, MOD_ROWS, d), lambda i, s: (s[i], 0, 0))],
        out_specs=pl.BlockSpec((tm, d), lambda i, s: (i, 0)))
    return pl.pallas_call(
        functools.partial(_norm_mod_kernel, shift_row=shift_row),
        grid_spec=grid_spec, out_shape=jax.ShapeDtypeStruct((n, d), BF16),
        compiler_params=_cparams(1), name="norm_mod")(seg, x, g, mod)


def _first_index(vals, m):
    idx = jnp.full(vals[0].shape, len(vals) - 1, I32)
    for k in reversed(range(len(vals) - 1)):
        idx = jnp.where(vals[k] >= m, k, idx)
    return idx


def _pick(vals, idx):
    out = vals[-1]
    for k in reversed(range(len(vals) - 1)):
        out = jnp.where(idx == k, vals[k], out)
    return out


def _max_list(vals):
    m = vals[0]
    for v in vals[1:]:
        m = jnp.maximum(m, v)
    return m


def _moe_pre_kernel(seg_ref, x_ref, g_ref, mod_ref, rwt_ref, rb_ref, tri_ref,
                    u_ref, e_ref, w_ref, r_ref, cnt_ref, base_ref, *, shift_row, n_groups):
    del seg_ref

    @pl.when(pl.program_id(0) == 0)
    def _():
        base_ref[...] = jnp.zeros(base_ref.shape, F32)

    u = _norm_mod(x_ref[...], g_ref[...], mod_ref, shift_row)
    u_ref[...] = u
    logits = lax.dot_general(rwt_ref[...], u, NT_DIMS, precision=HIGHEST,
                             preferred_element_type=F32)
    n_exp, tm = logits.shape
    epg = n_exp // n_groups
    ex = jnp.exp(logits - jnp.max(logits, axis=0, keepdims=True))
    probs = ex / jnp.sum(ex, axis=0, keepdims=True)
    sel = probs + rb_ref[...]
    sel_rows = [sel[e:e + 1, :] for e in range(n_exp)]
    p_rows = [probs[e:e + 1, :] for e in range(n_exp)]
    scores, i1s, i2s, p1s, p2s = [], [], [], [], []
    for gi in range(n_groups):
        v = sel_rows[gi * epg:(gi + 1) * epg]
        p = p_rows[gi * epg:(gi + 1) * epg]
        m1 = _max_list(v)
        i1 = _first_index(v, m1)
        v2 = [jnp.where(i1 == k, -jnp.inf, v[k]) for k in range(epg)]
        m2 = _max_list(v2)
        i2 = _first_index(v2, m2)
        scores.append(m1 + m2)
        i1s.append(i1)
        i2s.append(i2)
        p1s.append(_pick(p, i1))
        p2s.append(_pick(p, i2))
    gstar = _first_index(scores, _max_list(scores))
    e1 = gstar * epg + _pick(i1s, gstar)
    e2 = gstar * epg + _pick(i2s, gstar)
    p1 = _pick(p1s, gstar)
    p2 = _pick(p2s, gstar)
    tot = p1 + p2
    e_ref[0:1, :] = e1
    e_ref[1:2, :] = e2
    e_ref[2:, :] = jnp.zeros((MOD_ROWS - 2, tm), I32)
    w_ref[0:1, :] = p1 / tot
    w_ref[1:2, :] = p2 / tot
    w_ref[2:, :] = jnp.zeros((MOD_ROWS - 2, tm), F32)

    expert_id = lax.broadcasted_iota(I32, (n_exp, tm), 0)
    oh1 = (expert_id == e1).astype(F32)
    oh2 = (expert_id == e2).astype(F32)
    tri = tri_ref[...]
    cs1 = jnp.dot(oh1.astype(BF16), tri, preferred_element_type=F32)
    cs2 = jnp.dot(oh2.astype(BF16), tri, preferred_element_type=F32)
    base = base_ref[:, 0:1]
    c1 = cs1[:, tm - 1:tm]
    c2 = cs2[:, tm - 1:tm]
    r1 = jnp.sum(oh1 * (base + cs1), axis=0, keepdims=True) - 1.0
    r2 = jnp.sum(oh2 * (base + c1 + cs2), axis=0, keepdims=True) - 1.0
    r_ref[0:1, :] = r1.astype(I32)
    r_ref[1:2, :] = r2.astype(I32)
    r_ref[2:, :] = jnp.zeros((MOD_ROWS - 2, tm), I32)
    total = base + c1 + c2
    base_ref[...] = jnp.broadcast_to(total, base_ref.shape)
    cnt_ref[...] = jnp.broadcast_to(total, cnt_ref.shape)


def moe_pre(x, g, mod, seg, router_wt, router_b_col, tri, *, shift_row, tm):
    n, d = x.shape
    n_exp = router_wt.shape[0]
    grid_spec = pltpu.PrefetchScalarGridSpec(
        num_scalar_prefetch=1, grid=(n // tm,),
        in_specs=[pl.BlockSpec((tm, d), lambda i, s: (i, 0)),
                  pl.BlockSpec((1, d), lambda i, s: (0, 0)),
                  pl.BlockSpec((1, MOD_ROWS, d), lambda i, s: (s[i], 0, 0)),
                  pl.BlockSpec((n_exp, d), lambda i, s: (0, 0)),
                  pl.BlockSpec((n_exp, 1), lambda i, s: (0, 0)),
                  pl.BlockSpec((tm, tm), lambda i, s: (0, 0))],
        out_specs=[pl.BlockSpec((tm, d), lambda i, s: (i, 0)),
                   pl.BlockSpec((MOD_ROWS, tm), lambda i, s: (0, i)),
                   pl.BlockSpec((MOD_ROWS, tm), lambda i, s: (0, i)),
                   pl.BlockSpec((MOD_ROWS, tm), lambda i, s: (0, i)),
                   pl.BlockSpec((n_exp, LANES), lambda i, s: (0, 0))],
        scratch_shapes=[pltpu.VMEM((n_exp, LANES), F32)])
    return pl.pallas_call(
        functools.partial(_moe_pre_kernel, shift_row=shift_row, n_groups=N_GROUPS),
        grid_spec=grid_spec,
        out_shape=[jax.ShapeDtypeStruct((n, d), F32),
                   jax.ShapeDtypeStruct((MOD_ROWS, n), I32),
                   jax.ShapeDtypeStruct((MOD_ROWS, n), F32),
                   jax.ShapeDtypeStruct((MOD_ROWS, n), I32),
                   jax.ShapeDtypeStruct((n_exp, LANES), F32)],
        compiler_params=_cparams(1), name="moe_pre")(seg, x, g, mod, router_wt, router_b_col, tri)


def _linear_kernel(*refs, silu, has_bias, gate_row):
    refs = list(refs)
    refs.pop(0)
    a_ref, w_ref = refs.pop(0), refs.pop(0)
    bias_ref = refs.pop(0) if has_bias else None
    x_ref, mod_ref = (refs.pop(0), refs.pop(0)) if gate_row is not None else (None, None)
    o_ref, wb_ref = refs

    @pl.when(pl.program_id(1) == 0)
    def _():
        wb_ref[...] = w_ref[...].astype(BF16)

    a = a_ref[...]
    if silu:
        a = a * _sigmoid(a)
    acc = jnp.dot(a.astype(BF16), wb_ref[...], preferred_element_type=F32)
    if has_bias:
        acc = acc + bias_ref[...]
    if gate_row is not None:
        acc = x_ref[...] + mod_ref[0, gate_row:gate_row + 1, :] * acc
    o_ref[...] = acc.astype(o_ref.dtype)


def linear(a, w, seg, *, bias=None, resid=None, mod=None, gate_row=None, silu=False, tm, tn):
    m, k = a.shape
    n = w.shape[1]
    in_specs = [pl.BlockSpec((tm, k), lambda j, i, s: (i, 0)),
                pl.BlockSpec((k, tn), lambda j, i, s: (0, j))]
    args = [a, w]
    if bias is not None:
        in_specs.append(pl.BlockSpec((1, tn), lambda j, i, s: (0, j)))
        args.append(bias)
    if gate_row is not None:
        in_specs.append(pl.BlockSpec((tm, tn), lambda j, i, s: (i, j)))
        in_specs.append(pl.BlockSpec((1, MOD_ROWS, tn), lambda j, i, s: (s[i], 0, j)))
        args += [resid, mod]
    grid_spec = pltpu.PrefetchScalarGridSpec(
        num_scalar_prefetch=1, grid=(n // tn, m // tm), in_specs=in_specs,
        out_specs=pl.BlockSpec((tm, tn), lambda j, i, s: (i, j)),
        scratch_shapes=[pltpu.VMEM((k, tn), BF16)])
    return pl.pallas_call(
        functools.partial(_linear_kernel, silu=silu, has_bias=bias is not None, gate_row=gate_row),
        grid_spec=grid_spec, out_shape=jax.ShapeDtypeStruct((m, n), F32),
        compiler_params=_cparams(2), name="linear")(seg, *args)


def _qkv_kernel(*refs, rope, keep_f32, n_a, n_bq, scale):
    refs = list(refs)
    a_ref, w_ref, bd_ref, gains_ref = refs[:4]
    refs = refs[4:]
    if rope:
        c_ref, s_ref = refs[:2]
        refs = refs[2:]
    q_ref, k_ref, v_ref = refs[:3]
    refs = refs[3:]
    if keep_f32:
        kf_ref, vf_ref = refs[:2]
        refs = refs[2:]
    wb_ref, = refs

    @pl.when(pl.program_id(0) == 0)
    def _():
        wb_ref[...] = w_ref[...].astype(BF16)

    acc = jnp.dot(a_ref[...], wb_ref[...], preferred_element_type=F32)
    tm = acc.shape[0]
    bd = bd_ref[...]
    lane = lax.broadcasted_iota(I32, (tm, LANES), 1)
    low_half = lane < HEAD_DIM

    def headnorm(x, row):
        ms = jnp.dot((x * x).astype(BF16), bd, preferred_element_type=F32)
        return x * lax.rsqrt(ms + NORM_EPS) * gains_ref[row:row + 1, :]

    def rot(x):
        if not rope:
            return x
        partner = jnp.where((lane & ROPE_PAIRS) == 0,
                            pltpu.roll(x, LANES - ROPE_PAIRS, 1), pltpu.roll(x, ROPE_PAIRS, 1))
        return x * c_ref[...] + partner * s_ref[...]

    def chunk(col):
        return acc[:, col * LANES:(col + 1) * LANES]

    def dup(x):
        swapped = pltpu.roll(x, HEAD_DIM, 1)
        return jnp.where(low_half, x, swapped), jnp.where(low_half, swapped, x)

    for c in range(n_a):
        q_ref[:, c * LANES:(c + 1) * LANES] = (rot(headnorm(chunk(c), 0)) * scale).astype(BF16)
        kn = headnorm(chunk(n_a + c), 1)
        k_ref[:, c * LANES:(c + 1) * LANES] = rot(kn).astype(BF16)
        vv = chunk(2 * n_a + c)
        v_ref[:, c * LANES:(c + 1) * LANES] = vv.astype(BF16)
        if keep_f32:
            kf_ref[:, c * LANES:(c + 1) * LANES] = kn
            vf_ref[:, c * LANES:(c + 1) * LANES] = vv
    for c in range(n_bq):
        q_ref[:, (n_a + c) * LANES:(n_a + c + 1) * LANES] = (
            rot(headnorm(chunk(3 * n_a + c), 2)) * scale).astype(BF16)
    kb = headnorm(chunk(3 * n_a + n_bq), 3)
    vb = chunk(3 * n_a + n_bq + 1)
    k0, k1 = dup(rot(kb))
    v0, v1 = dup(vb)
    k_ref[:, n_a * LANES:(n_a + 1) * LANES] = k0.astype(BF16)
    k_ref[:, (n_a + 1) * LANES:(n_a + 2) * LANES] = k1.astype(BF16)
    v_ref[:, n_a * LANES:(n_a + 1) * LANES] = v0.astype(BF16)
    v_ref[:, (n_a + 1) * LANES:(n_a + 2) * LANES] = v1.astype(BF16)
    if keep_f32:
        kf_ref[:, n_a * LANES:(n_a + 1) * LANES] = kb
        vf_ref[:, n_a * LANES:(n_a + 1) * LANES] = vb


def qkv_project(u, w, bd, gains, rope_cs, *, keep_f32, n_a, n_bq, tm):
    m, d = u.shape
    n = w.shape[1]
    assert n == (3 * n_a + n_bq + 2) * LANES
    rope = rope_cs is not None
    in_specs = [pl.BlockSpec((tm, d), lambda i: (i, 0)),
                pl.BlockSpec((d, n), lambda i: (0, 0)),
                pl.BlockSpec((LANES, LANES), lambda i: (0, 0)),
                pl.BlockSpec((MOD_ROWS, LANES), lambda i: (0, 0))]
    args = [u, w, bd, gains]
    if rope:
        rows = rope_cs[0].shape[0] // tm
        in_specs += [pl.BlockSpec((tm, LANES), lambda i: (i % rows, 0))] * 2
        args += list(rope_cs)
    qw, kw = (n_a + n_bq) * LANES, (n_a + 2) * LANES
    out_specs = [pl.BlockSpec((tm, qw), lambda i: (i, 0)),
                 pl.BlockSpec((tm, kw), lambda i: (i, 0)),
                 pl.BlockSpec((tm, kw), lambda i: (i, 0))]
    out_shape = [jax.ShapeDtypeStruct((m, qw), BF16), jax.ShapeDtypeStruct((m, kw), BF16),
                 jax.ShapeDtypeStruct((m, kw), BF16)]
    if keep_f32:
        fw = (n_a + 1) * LANES
        out_specs += [pl.BlockSpec((tm, fw), lambda i: (i, 0))] * 2
        out_shape += [jax.ShapeDtypeStruct((m, fw), F32)] * 2
    return pl.pallas_call(
        functools.partial(_qkv_kernel, rope=rope, keep_f32=keep_f32, n_a=n_a, n_bq=n_bq,
                          scale=HEAD_DIM ** -0.5 * math.log2(math.e)),
        grid=(m // tm,), in_specs=in_specs, out_specs=out_specs, out_shape=out_shape,
        scratch_shapes=[pltpu.VMEM((d, n), BF16)],
        compiler_params=_cparams(1), name="qkv")(*args)


def _attn_kernel(q_ref, k_ref, v_ref, lam_ref, sg_ref, o_ref, sa_ref, sb_ref, ma_ref, mb_ref,
                 *, n_diff, out_scale, tk, n_chunks, n_qt):
    t = pl.program_id(0)
    tq = q_ref.shape[0]
    n_blk = k_ref.shape[0] // tk
    lane = lax.broadcasted_iota(I32, (tq, LANES), 1)
    low_half = lane < HEAD_DIM
    c_prev = (jnp.maximum(t - 1, 0) // n_qt) % n_chunks

    @pl.when(t == 0)
    def _():
        sb_ref[...] = jnp.zeros(sb_ref.shape, F32)
        mb_ref[...] = jnp.zeros(mb_ref.shape, F32)

    def step(s_cur, m_cur, s_prev, m_prev):
        qf = q_ref[...].astype(F32)
        q01 = jnp.concatenate([jnp.where(low_half, qf, 0.0), jnp.where(low_half, 0.0, qf)],
                              axis=0).astype(BF16)
        mx = None
        for j in range(n_blk):
            s = lax.dot_general(q01, k_ref[j * tk:(j + 1) * tk, :], NT_DIMS,
                                preferred_element_type=F32)
            s_cur[:, j * tk:(j + 1) * tk] = s
            for u in range(tk // LANES):
                blk = s[:, u * LANES:(u + 1) * LANES]
                mx = blk if mx is None else jnp.maximum(mx, blk)
        m_cur[...] = jnp.max(mx, axis=-1, keepdims=True)

        m = m_prev[...]
        lsum = jnp.zeros((2 * tq, LANES), F32)
        acc = jnp.zeros((2 * tq, LANES), F32)
        for j in range(n_blk):
            p = jnp.exp2(s_prev[:, j * tk:(j + 1) * tk] - m)
            for u in range(tk // LANES):
                lsum = lsum + p[:, u * LANES:(u + 1) * LANES]
            acc = acc + jnp.dot(p.astype(BF16), v_ref[j * tk:(j + 1) * tk, :],
                                preferred_element_type=F32)
        o = acc / jnp.sum(lsum, axis=-1, keepdims=True)
        o0, o1 = o[:tq], o[tq:]
        od = o0 - lam_ref[...] * o1
        ms = jnp.mean(od * od, axis=-1, keepdims=True)
        od = od * lax.rsqrt(ms + NORM_EPS) * sg_ref[...] * out_scale
        og = jnp.where(low_half, o0, o1)
        o_ref[...] = jnp.where(c_prev < n_diff, od, og).astype(o_ref.dtype)

    @pl.when(t % 2 == 0)
    def _():
        step(sa_ref, ma_ref, sb_ref, mb_ref)

    @pl.when(t % 2 == 1)
    def _():
        step(sb_ref, mb_ref, sa_ref, ma_ref)


def attention(q, k, v, lam_row, subln_row, *, n_diff, out_scale, tq, tk):
    b, lq, qw = q.shape
    s = k.shape[1]
    n_chunks = qw // LANES
    n_qt = lq // tq
    n_tiles = b * n_chunks * n_qt

    def tile(n):
        return n // (n_chunks * n_qt), (n // n_qt) % n_chunks, n % n_qt

    def kv_chunk(c):
        return jnp.where(c < n_diff, c, n_diff + (c - n_diff) // 2)

    def q_map(t):
        bi, c, i = tile(jnp.minimum(t, n_tiles - 1))
        return bi, i, c

    def k_map(t):
        bi, c, _ = tile(jnp.minimum(t, n_tiles - 1))
        return bi, 0, kv_chunk(c)

    def v_map(t):
        bi, c, _ = tile(jnp.maximum(t - 1, 0))
        return bi, 0, kv_chunk(c)

    def o_map(t):
        bi, c, i = tile(jnp.maximum(t - 1, 0))
        return bi, i, c

    return pl.pallas_call(
        functools.partial(_attn_kernel, n_diff=n_diff, out_scale=out_scale, tk=tk,
                          n_chunks=n_chunks, n_qt=n_qt),
        grid=(n_tiles + 1,),
        in_specs=[pl.BlockSpec((None, tq, LANES), q_map),
                  pl.BlockSpec((None, s, LANES), k_map),
                  pl.BlockSpec((None, s, LANES), v_map),
                  pl.BlockSpec((1, LANES), lambda t: (0, 0)),
                  pl.BlockSpec((1, LANES), lambda t: (0, 0))],
        out_specs=pl.BlockSpec((None, tq, LANES), o_map),
        out_shape=jax.ShapeDtypeStruct((b, lq, qw), BF16),
        scratch_shapes=[pltpu.VMEM((2 * tq, s), F32), pltpu.VMEM((2 * tq, s), F32),
                        pltpu.VMEM((2 * tq, 1), F32), pltpu.VMEM((2 * tq, 1), F32)],
        compiler_params=_cparams(1), name="attention")(q, k, v, lam_row, subln_row)


DMA_UNROLL = 8


def _row_copy(src_ref, row, dst_ref, dst_row, sem):
    return pltpu.make_async_copy(src_ref.at[pl.ds(row, 1), :], dst_ref.at[pl.ds(dst_row, 1), :], sem)


def _dispatch_kernel(slot_ref, h_ref, xs_in_hbm, xs_hbm, sem):
    del xs_in_hbm
    rows = h_ref.shape[0]

    def start(r, carry):
        for kk in range(2):
            _row_copy(h_ref, r, xs_hbm, slot_ref[0, 0, kk * rows + r], sem).start()
        return carry

    def wait(r, carry):
        for kk in range(2):
            _row_copy(h_ref, r, xs_hbm, 0, sem).wait()
        return carry

    lax.fori_loop(0, rows, start, 0, unroll=DMA_UNROLL)
    lax.fori_loop(0, rows, wait, 0, unroll=DMA_UNROLL)


def moe_dispatch(h, slots, xs_zero, *, td):
    n, d = h.shape
    tiles = n // td
    slot_tiles = slots.reshape(2, tiles, td).transpose(1, 0, 2).reshape(tiles, 1, 2 * td)
    return pl.pallas_call(
        _dispatch_kernel, grid=(tiles,),
        in_specs=[pl.BlockSpec((1, 1, 2 * td), lambda i: (i, 0, 0), memory_space=pltpu.SMEM),
                  pl.BlockSpec((td, d), lambda i: (i, 0)),
                  pl.BlockSpec(memory_space=pl.ANY)],
        out_specs=pl.BlockSpec(memory_space=pl.ANY),
        out_shape=jax.ShapeDtypeStruct(xs_zero.shape, xs_zero.dtype),
        input_output_aliases={2: 0},
        scratch_shapes=[pltpu.SemaphoreType.DMA(())],
        compiler_params=_cparams(1), name="moe_dispatch")(slot_tiles, h, xs_zero)


def _ffn_kernel(te_ref, tv_ref, x_ref, wg_ref, wu_ref, wd_ref, o_ref, wgb_ref, wub_ref, wdb_ref):
    t = pl.program_id(0)
    prev = te_ref[jnp.maximum(t - 1, 0)]
    new_expert = jnp.logical_or(t == 0, te_ref[t] != prev)

    @pl.when(new_expert)
    def _():
        wgb_ref[...] = wg_ref[...].astype(BF16)
        wub_ref[...] = wu_ref[...].astype(BF16)
        wdb_ref[...] = wd_ref[...].astype(BF16)

    @pl.when(tv_ref[t] != 0)
    def _():
        x = x_ref[...].astype(BF16)
        gate = jnp.dot(x, wgb_ref[...], preferred_element_type=F32)
        up = jnp.dot(x, wub_ref[...], preferred_element_type=F32)
        h = (gate * _sigmoid(gate) * up).astype(BF16)
        o_ref[...] = jnp.dot(h, wdb_ref[...], preferred_element_type=F32)

    @pl.when(tv_ref[t] == 0)
    def _():
        o_ref[...] = jnp.zeros(o_ref.shape, o_ref.dtype)


def moe_ffn(xs, tile_expert, tile_valid, wg, wu, wd, *, tm):
    p, d = xs.shape
    f = wg.shape[2]
    grid_spec = pltpu.PrefetchScalarGridSpec(
        num_scalar_prefetch=2, grid=(p // tm,),
        in_specs=[pl.BlockSpec((tm, d), lambda t, te, tv: (t, 0)),
                  pl.BlockSpec((None, d, f), lambda t, te, tv: (te[t], 0, 0)),
                  pl.BlockSpec((None, d, f), lambda t, te, tv: (te[t], 0, 0)),
                  pl.BlockSpec((None, f, d), lambda t, te, tv: (te[t], 0, 0))],
        out_specs=pl.BlockSpec((tm, d), lambda t, te, tv: (t, 0)),
        scratch_shapes=[pltpu.VMEM((d, f), BF16), pltpu.VMEM((d, f), BF16),
                        pltpu.VMEM((f, d), BF16)])
    return pl.pallas_call(
        _ffn_kernel, grid_spec=grid_spec, out_shape=jax.ShapeDtypeStruct((p, d), F32),
        compiler_params=_cparams(1), name="moe_ffn")(tile_expert, tile_valid, xs, wg, wu, wd)


def _combine_kernel(seg_ref, slot_ref, y_hbm, x_ref, w0_ref, w1_ref, mod_ref, o_ref, buf_ref, sem,
                    *, gate_row):
    del seg_ref
    rows = x_ref.shape[0]

    def start(r, carry):
        for kk in range(2):
            _row_copy(y_hbm, slot_ref[0, 0, kk * rows + r], buf_ref.at[kk], r, sem).start()
        return carry

    def wait(r, carry):
        for kk in range(2):
            _row_copy(y_hbm, 0, buf_ref.at[kk], r, sem).wait()
        return carry

    lax.fori_loop(0, rows, start, 0, unroll=DMA_UNROLL)
    lax.fori_loop(0, rows, wait, 0, unroll=DMA_UNROLL)
    moe = w0_ref[...] * buf_ref[0] + w1_ref[...] * buf_ref[1]
    o_ref[...] = x_ref[...] + mod_ref[0, gate_row:gate_row + 1, :] * moe


def moe_combine(y, slots, x, w0, w1, mod, seg, *, gate_row, tc):
    n, d = x.shape
    tiles = n // tc
    slot_tiles = slots.reshape(2, tiles, tc).transpose(1, 0, 2).reshape(tiles, 1, 2 * tc)
    grid_spec = pltpu.PrefetchScalarGridSpec(
        num_scalar_prefetch=1, grid=(n // tc,),
        in_specs=[pl.BlockSpec((1, 1, 2 * tc), lambda i, s: (i, 0, 0), memory_space=pltpu.SMEM),
                  pl.BlockSpec(memory_space=pl.ANY),
                  pl.BlockSpec((tc, d), lambda i, s: (i, 0)),
                  pl.BlockSpec((tc, 1), lambda i, s: (i, 0)),
                  pl.BlockSpec((tc, 1), lambda i, s: (i, 0)),
                  pl.BlockSpec((1, MOD_ROWS, d), lambda i, s: (s[i], 0, 0))],
        out_specs=pl.BlockSpec((tc, d), lambda i, s: (i, 0)),
        scratch_shapes=[pltpu.VMEM((2, tc, d), F32), pltpu.SemaphoreType.DMA(())])
    return pl.pallas_call(
        functools.partial(_combine_kernel, gate_row=gate_row),
        grid_spec=grid_spec, out_shape=jax.ShapeDtypeStruct((n, d), F32),
        compiler_params=_cparams(1), name="moe_combine")(seg, slot_tiles, y, x, w0, w1, mod)


def _route(e_idx, rank, counts, *, tm, p_max):
    n_exp = counts.shape[0]
    padded = ((counts + tm - 1) // tm) * tm
    ends = jnp.cumsum(padded)
    starts = ends - padded
    slots = jnp.take(starts, e_idx) + rank
    tile_start = jnp.arange(p_max // tm, dtype=I32) * tm
    tile_expert = jnp.minimum(jnp.sum((tile_start[:, None] >= ends[None, :]).astype(I32), axis=1),
                              n_exp - 1)
    tile_valid = (tile_start < ends[-1]).astype(I32)
    return slots, tile_expert, tile_valid


def _linear_nt_kernel(wt_ref, u_ref, bias_ref, o_ref, wb_ref):
    @pl.when(jnp.logical_and(pl.program_id(1) == 0, pl.program_id(2) == 0))
    def _():
        wb_ref[...] = wt_ref[...].astype(BF16)

    acc = lax.dot_general(wb_ref[...], u_ref[...], NT_DIMS, preferred_element_type=F32)
    o_ref[...] = acc + bias_ref[...]


def linear_nt(wt, u, bias_col, *, tn, tl):
    n, k = wt.shape
    b, l, _ = u.shape
    return pl.pallas_call(
        _linear_nt_kernel, grid=(n // tn, b, l // tl),
        in_specs=[pl.BlockSpec((tn, k), lambda j, bi, i: (j, 0)),
                  pl.BlockSpec((None, tl, k), lambda j, bi, i: (bi, i, 0)),
                  pl.BlockSpec((tn, 1), lambda j, bi, i: (j, 0))],
        out_specs=pl.BlockSpec((None, tn, tl), lambda j, bi, i: (bi, j, i)),
        out_shape=jax.ShapeDtypeStruct((b, n, l), F32),
        scratch_shapes=[pltpu.VMEM((tn, k), BF16)],
        compiler_params=_cparams(3), name="linear_nt")(wt, u, bias_col)


def _filter_mlp_kernel(z_ref, w1_ref, b1_ref, f1_ref, w2_ref, b2_ref, f2_ref, o_ref):
    h = jnp.dot(w1_ref[...], z_ref[...], precision=HIGHEST, preferred_element_type=F32)
    h = jnp.sin(f1_ref[...] * (h + b1_ref[...]))
    h = jnp.dot(w2_ref[...], h, precision=HIGHEST, preferred_element_type=F32)
    o_ref[...] = jnp.sin(f2_ref[...] * (h + b2_ref[...]))


def filter_mlp(zt, w1t, b1, f1, w2t, b2, f2):
    width, l = w2t.shape[0], zt.shape[1]
    return pl.pallas_call(
        _filter_mlp_kernel, out_shape=jax.ShapeDtypeStruct((width, l), F32),
        compiler_params=pltpu.CompilerParams(vmem_limit_bytes=VMEM_LIMIT_BYTES),
        name="hyena_filter_mlp")(zt, w1t, b1, f1, w2t, b2, f2)


def _filter_out_kernel(w3_ref, h_ref, t_ref, delta_ref, o_ref):
    acc = jnp.dot(w3_ref[...].astype(BF16), h_ref[...].astype(BF16), preferred_element_type=F32)
    o_ref[...] = acc * jnp.exp(-t_ref[...] * delta_ref[...])


def filter_out(w3t, h2t, t_row, delta_col, *, tr):
    rows, width = w3t.shape
    l = h2t.shape[1]
    d_tiles = delta_col.shape[0] // tr
    return pl.pallas_call(
        _filter_out_kernel, grid=(rows // tr,),
        in_specs=[pl.BlockSpec((tr, width), lambda i: (i, 0)),
                  pl.BlockSpec((width, l), lambda i: (0, 0)),
                  pl.BlockSpec((1, l), lambda i: (0, 0)),
                  pl.BlockSpec((tr, 1), lambda i: (i % d_tiles, 0))],
        out_specs=pl.BlockSpec((tr, l), lambda i: (i, 0)),
        out_shape=jax.ShapeDtypeStruct((rows, l), F32),
        compiler_params=_cparams(1), name="hyena_filter_out")(w3t, h2t, t_row, delta_col)


def _spectrum_kernel(a_ref, b_ref, wf_ref, sgn_ref, o_ref):
    a = a_ref[...]
    lane = lax.broadcasted_iota(I32, a.shape, 1)
    a = jnp.where(lane == 0, 0.0, a)
    wf = wf_ref[...]
    ta = jnp.dot(a.astype(BF16), wf, preferred_element_type=F32)
    tb = jnp.dot(b_ref[...].astype(BF16), wf, preferred_element_type=F32)
    o_ref[...] = sgn_ref[...] * ta + tb


def filter_spectra(f_full, wf, sgn, *, p, tr):
    n_ord, d, two_l = f_full.shape
    nseg = two_l // p - 1
    return pl.pallas_call(
        _spectrum_kernel, grid=(n_ord, d // tr, nseg),
        in_specs=[pl.BlockSpec((None, tr, p), lambda o, i, j: (o, i, j)),
                  pl.BlockSpec((None, tr, p), lambda o, i, j: (o, i, j + 1)),
                  pl.BlockSpec((p, 2 * p), lambda o, i, j: (0, 0)),
                  pl.BlockSpec((1, 2 * p), lambda o, i, j: (0, 0))],
        out_specs=pl.BlockSpec((None, None, tr, 2 * p), lambda o, i, j: (o, j, i, 0)),
        out_shape=jax.ShapeDtypeStruct((n_ord, nseg, d, 2 * p), F32),
        compiler_params=_cparams(3), name="hyena_filter_spectra")(f_full, f_full, wf, sgn)


def _hyena_conv_kernel(pv_ref, p1_ref, p2_ref, cv_ref, c1_ref, c2_ref, sk_ref, g_ref,
                       wf_ref, wi_ref, o_ref, *, p, nb):
    td, l = pv_ref.shape
    lane_l = lax.broadcasted_iota(I32, (td, l), 1)
    lane_p = lax.broadcasted_iota(I32, (td, p), 1)
    dc = lane_p == 0

    def short_conv(x_ref, cw_ref):
        x = x_ref[...]
        cw = cw_ref[...]
        prev = jnp.where(lane_l == 0, 0.0, pltpu.roll(x, 1, 1))
        nxt = jnp.where(lane_l == l - 1, 0.0, pltpu.roll(x, l - 1, 1))
        return prev * cw[:, 0:1] + x * cw[:, 1:2] + nxt * cw[:, 2:3] + cw[:, 3:4]

    def long_conv(z, order):
        zb = jnp.concatenate([z[:, j * p:(j + 1) * p] for j in range(nb)], axis=0).astype(BF16)
        zh = jnp.dot(zb, wf_ref[...], preferred_element_type=F32)
        blocks = []
        for i in range(nb):
            t_rr = t_ii = t_ri = t_ir = None
            for j in range(nb):
                zr = zh[j * td:(j + 1) * td, :p]
                zi = zh[j * td:(j + 1) * td, p:]
                g = g_ref[order, i - j + nb - 1]
                gr, gi = g[:, :p], g[:, p:]
                if t_rr is None:
                    t_rr, t_ii, t_ri, t_ir = zr * gr, zi * gi, zr * gi, zi * gr
                else:
                    t_rr, t_ii = t_rr + zr * gr, t_ii + zi * gi
                    t_ri, t_ir = t_ri + zr * gi, t_ir + zi * gr
            yr = t_rr - jnp.where(dc, 0.0, t_ii)
            yi = jnp.where(dc, t_ii, t_ri + t_ir)
            blocks.append(jnp.concatenate([yr, yi], axis=1))
        yh = jnp.concatenate(blocks, axis=0).astype(BF16)
        y = jnp.dot(yh, wi_ref[...], preferred_element_type=F32)
        return jnp.concatenate([y[i * td:(i + 1) * td] for i in range(nb)], axis=1)

    v = short_conv(pv_ref, cv_ref)
    sk = sk_ref[...]
    z1 = short_conv(p1_ref, c1_ref) * (long_conv(v, 0) + v * sk[:, 0:1])
    o_ref[...] = short_conv(p2_ref, c2_ref) * (long_conv(z1, 1) + z1 * sk[:, 1:2])


def hyena_conv(pt, cwb, sk, g, wf, wi, *, p, td):
    b, d3, l = pt.shape
    d = d3 // 3
    nb = l // p
    n_ord, nseg = g.shape[:2]
    dt = d // td
    return pl.pallas_call(
        functools.partial(_hyena_conv_kernel, p=p, nb=nb),
        grid=(dt, b),
        in_specs=[pl.BlockSpec((None, td, l), lambda i, bi: (bi, i, 0)),
                  pl.BlockSpec((None, td, l), lambda i, bi: (bi, dt + i, 0)),
                  pl.BlockSpec((None, td, l), lambda i, bi: (bi, 2 * dt + i, 0)),
                  pl.BlockSpec((td, MOD_ROWS), lambda i, bi: (i, 0)),
                  pl.BlockSpec((td, MOD_ROWS), lambda i, bi: (dt + i, 0)),
                  pl.BlockSpec((td, MOD_ROWS), lambda i, bi: (2 * dt + i, 0)),
                  pl.BlockSpec((td, MOD_ROWS), lambda i, bi: (i, 0)),
                  pl.BlockSpec((n_ord, nseg, td, 2 * p), lambda i, bi: (0, 0, i, 0)),
                  pl.BlockSpec((p, 2 * p), lambda i, bi: (0, 0)),
                  pl.BlockSpec((2 * p, p), lambda i, bi: (0, 0))],
        out_specs=pl.BlockSpec((None, td, l), lambda i, bi: (bi, i, 0)),
        out_shape=jax.ShapeDtypeStruct((b, d, l), F32),
        compiler_params=_cparams(2), name="hyena_conv")(pt, pt, pt, cwb, cwb, cwb, sk, g, wf, wi)


def _dft_tables(p):
    n = jnp.arange(p, dtype=I32)
    ang = (jnp.pi / p) * ((n[:, None] * n[None, :]) % (2 * p)).astype(F32)
    cos, sin = jnp.cos(ang), jnp.sin(ang)
    alt = jnp.where(n % 2 == 0, 1.0, -1.0).astype(F32)
    is_dc = (n == 0)
    fwd_im = jnp.where(is_dc[None, :], alt[:, None], -sin)
    wf = jnp.concatenate([cos, fwd_im], axis=1)
    inv_re = jnp.where(is_dc[:, None], 0.5 / p, cos / p)
    inv_im = jnp.where(is_dc[:, None], (0.5 / p) * alt[None, :], -sin / p)
    wi = jnp.concatenate([inv_re, inv_im], axis=0)
    sgn = jnp.concatenate([alt, alt])[None, :]
    return wf.astype(BF16), wi.astype(BF16), sgn


def _hyena_filter_spectra(l, p, prm, wf, sgn, d):
    f_w1, f_b1, f_fr1, f_w2, f_b2, f_fr2, f_w3 = prm
    emb = f_w1.shape[0]
    bands_n = (emb - 1) // 2
    pos = jnp.arange(l, dtype=F32)[None, :]
    t = jnp.linspace(0.0, 1.0, l, dtype=F32)[None, :]
    bands = jnp.linspace(1e-4, bands_n - 1, bands_n, dtype=F32)[:, None]
    ang = (2.0 * math.pi / l) * pos * bands
    emb_pad = -(-emb // LANES) * LANES
    zt = jnp.concatenate([t, jnp.cos(ang), -jnp.sin(ang), jnp.zeros((emb_pad - emb, l), F32)], axis=0)
    w1t = jnp.pad(f_w1.T, ((0, 0), (0, emb_pad - emb)))
    h2t = filter_mlp(zt, w1t, f_b1[:, None], f_fr1[:, None], f_w2.T, f_b2[:, None], f_fr2[:, None])
    deltas = jnp.abs(jnp.linspace(HY_MIN_DECAY, HY_MAX_DECAY, d, dtype=F32))[:, None]
    ht = filter_out(f_w3.T, h2t, t, deltas, tr=256)
    n_ord = ht.shape[0] // (2 * d)
    ht = ht.reshape(n_ord, 2, d, l)
    f_full = jnp.concatenate([jnp.zeros((n_ord, d, 1), F32), jnp.flip(ht[:, 1, :, 1:], axis=-1),
                              ht[:, 0]], axis=-1)
    return filter_spectra(f_full, wf, sgn, p=p, tr=256)


def _rope_tables(n_tok):
    rows = n_tok // GRID_W
    row = jnp.repeat(jnp.arange(rows), GRID_W).astype(F32)
    col = jnp.tile(jnp.arange(GRID_W), rows).astype(F32)
    inv = ROPE_THETA ** (-jnp.arange(ROPE_PAIRS, dtype=F32) / ROPE_PAIRS)
    ang_r, ang_c = row[:, None] * inv, col[:, None] * inv
    cos_h = jnp.concatenate([jnp.cos(ang_r)] * 2 + [jnp.cos(ang_c)] * 2, axis=-1)
    sin_h = jnp.concatenate([-jnp.sin(ang_r), jnp.sin(ang_r), -jnp.sin(ang_c), jnp.sin(ang_c)],
                            axis=-1)
    reps = LANES // HEAD_DIM
    return jnp.tile(cos_h, (1, reps)), jnp.tile(sin_h, (1, reps))


def _dup_heads(x):
    h0, h1 = x[..., :HEAD_DIM], x[..., HEAD_DIM:]
    return jnp.concatenate([h0, h0, h1, h1], axis=-1)


def kernel(x_prompt, x_sample, cache_a_k, cache_a_v, cache_b_k, cache_b_v, c, c_ctx,
           router_w, router_b, ada_w, ada_b, norm1_g, norm2_g, w_qkv, w_o,
           qn_a, kn_a, qn_b, kn_b, lam_q1, lam_k1, lam_q2, lam_k2, subln_a,
           hy_w_in, hy_b_in, hy_conv_w, hy_conv_b, hy_f_w1, hy_f_b1, hy_f_freq1,
           hy_f_w2, hy_f_b2, hy_f_freq2, hy_f_w3, hy_bias, hy_w_out, hy_b_out,
           moe_wg, moe_wu, moe_wd):
    nb_p, seq, d = x_prompt.shape
    nb_s, dec_seq, _ = x_sample.shape
    past = cache_a_k.shape[2]
    depth = ada_w.shape[0]
    n_exp = router_w.shape[1]
    n_p, n_s = nb_p * seq, nb_s * dec_seq
    n_tok = n_p + n_s
    n_a = cache_a_v.shape[3]
    n_bq = (w_qkv.shape[2] // LANES) - 3 * n_a - 2
    assert cache_b_k.shape[3] * HEAD_DIM == LANES and n_bq == 2 * (cache_b_k.shape[3])

    tm = 512
    assert n_p % tm == 0 and dec_seq % tm == 0
    seg_rows = np.concatenate([np.zeros(n_p, np.int32),
                               1 + np.arange(n_s, dtype=np.int32) // dec_seq])
    seg = jnp.asarray(seg_rows[::tm])
    tc = 256
    seg_c = jnp.asarray(seg_rows[::tc])
    seg0 = jnp.zeros((1,), I32)

    x = jnp.concatenate([x_prompt.reshape(n_p, d), x_sample.reshape(n_s, d)], axis=0)
    cvec = jnp.concatenate([c_ctx[None, :], c, jnp.zeros((16 - 1 - nb_s, d), F32)], axis=0)
    bd = (jnp.arange(LANES)[:, None] // HEAD_DIM == jnp.arange(LANES)[None, :] // HEAD_DIM)
    bd = (bd.astype(F32) / HEAD_DIM).astype(BF16)
    rope_cs = _rope_tables(dec_seq)
    router_wt = router_w.T
    router_b_col = router_b[:, None]
    tri = (jnp.arange(tm)[:, None] <= jnp.arange(tm)[None, :]).astype(BF16)
    reps = LANES // HEAD_DIM

    new_ak, new_av, new_bk, new_bv = [], [], [], []
    for l in range(depth):
        m = linear(cvec, ada_w[l], seg0, bias=ada_b[l][None, :], silu=True, tm=16, tn=1024)
        mod = jnp.pad(m[:1 + nb_s].reshape(1 + nb_s, 6, d), ((0, 0), (0, MOD_ROWS - 6), (0, 0)))
        u = norm_mod(x, norm1_g[l][None, :], mod, seg, shift_row=0, tm=tm)
        if l % 2 == 0:
            i = l // 2
            lam_init = 0.8 - 0.6 * math.exp(-0.3 * l)
            lam = (jnp.exp(jnp.sum(lam_q1[i] * lam_k1[i])) - jnp.exp(jnp.sum(lam_q2[i] * lam_k2[i]))
                   + lam_init)
            lam_row = jnp.full((1, LANES), lam, F32)
            gains = jnp.concatenate(
                [jnp.tile(gn[i], reps)[None, :] for gn in (qn_a, kn_a, qn_b, kn_b)]
                + [jnp.zeros((MOD_ROWS - 4, LANES), F32)], axis=0)
            sub_row = subln_a[i][None, :]
            qp, kp, vp, kf, vf = qkv_project(u[:n_p], w_qkv[i], bd, gains, None,
                                             keep_f32=True, n_a=n_a, n_bq=n_bq, tm=256)
            aw = n_a * LANES
            new_ak.append(kf[:, :aw].reshape(nb_p, seq, 2 * n_a, HEAD_DIM))
            new_av.append(vf[:, :aw].reshape(nb_p, seq, n_a, 2 * HEAD_DIM))
            new_bk.append(kf[:, aw:].reshape(nb_p, seq, LANES // HEAD_DIM, HEAD_DIM))
            new_bv.append(vf[:, aw:].reshape(nb_p, seq, LANES // HEAD_DIM, HEAD_DIM))
            op = attention(qp.reshape(nb_p, seq, -1), kp.reshape(nb_p, seq, -1),
                           vp.reshape(nb_p, seq, -1), lam_row, sub_row,
                           n_diff=n_a, out_scale=1.0 - lam_init, tq=seq, tk=seq)
            qs, ks, vs = qkv_project(u[n_p:], w_qkv[i], bd, gains, rope_cs,
                                     keep_f32=False, n_a=n_a, n_bq=n_bq, tm=256)
            ck = jnp.concatenate([cache_a_k[:, i].reshape(nb_s, past, aw),
                                  _dup_heads(cache_b_k[:, i].reshape(nb_s, past, LANES))],
                                 axis=-1).astype(BF16)
            cv = jnp.concatenate([cache_a_v[:, i].reshape(nb_s, past, aw),
                                  _dup_heads(cache_b_v[:, i].reshape(nb_s, past, LANES))],
                                 axis=-1).astype(BF16)
            k_full = jnp.concatenate([ck, ks.reshape(nb_s, dec_seq, -1)], axis=1)
            v_full = jnp.concatenate([cv, vs.reshape(nb_s, dec_seq, -1)], axis=1)
            os_ = attention(qs.reshape(nb_s, dec_seq, -1), k_full, v_full, lam_row, sub_row,
                            n_diff=n_a, out_scale=1.0 - lam_init, tq=256, tk=512)
            o = jnp.concatenate([op.reshape(n_p, -1), os_.reshape(n_s, -1)], axis=0)
            x = linear(o, w_o[i], seg, resid=x, mod=mod, gate_row=2, tm=tm, tn=d)
        else:
            j = l // 2
            w_in_t = hy_w_in[j].T
            b_in_col = hy_b_in[j][:, None]
            cwb = jnp.concatenate([hy_conv_w[j].T, hy_conv_b[j][:, None],
                                   jnp.zeros((3 * d, MOD_ROWS - 4), F32)], axis=1)
            sk = jnp.concatenate([hy_bias[j].T, jnp.zeros((d, MOD_ROWS - 2), F32)], axis=1)
            fprm = (hy_f_w1[j], hy_f_b1[j], hy_f_freq1[j], hy_f_w2[j], hy_f_b2[j], hy_f_freq2[j],
                    hy_f_w3[j])
            zs = []
            for ub, nbat, ln, p, td, tl in ((u[:n_p], nb_p, seq, seq, 256, seq),
                                            (u[n_p:], nb_s, dec_seq, 512, 64, 512)):
                wf, wi, sgn = _dft_tables(p)
                g = _hyena_filter_spectra(ln, p, fprm, wf, sgn, d)
                pt = linear_nt(w_in_t, ub.reshape(nbat, ln, d), b_in_col, tn=512, tl=tl)
                zt = hyena_conv(pt, cwb, sk, g, wf, wi, p=p, td=td)
                zs.append(jnp.transpose(zt, (0, 2, 1)).reshape(nbat * ln, d).astype(BF16))
            z = jnp.concatenate(zs, axis=0)
            x = linear(z, hy_w_out[j], seg, bias=hy_b_out[j][None, :], resid=x, mod=mod,
                       gate_row=2, tm=tm, tn=d)
        h, e_t, w_t, r_t, cnt = moe_pre(x, norm2_g[l][None, :], mod, seg, router_wt, router_b_col,
                                        tri, shift_row=3, tm=tm)
        tmg = 256
        p_max = 2 * n_tok + n_exp * tmg
        slots, tile_expert, tile_valid = _route(e_t[:2], r_t[:2], cnt[:, 0].astype(I32),
                                                tm=tmg, p_max=p_max)
        xs = moe_dispatch(h, slots, jnp.zeros((p_max, d), F32), td=tm)
        y = moe_ffn(xs, tile_expert, tile_valid, moe_wg[l], moe_wu[l], moe_wd[l], tm=tmg)
        x = moe_combine(y, slots, x, w_t[0][:, None], w_t[1][:, None], mod, seg_c, gate_row=5, tc=tc)

    y_prompt = x[:n_p].reshape(nb_p, seq, d)
    y_sample = x[n_p:].reshape(nb_s, dec_seq, d)
    return (y_prompt, y_sample, jnp.stack(new_ak, axis=1), jnp.stack(new_av, axis=1),
            jnp.stack(new_bk, axis=1), jnp.stack(new_bv, axis=1))
```

```python
import functools
import math

import numpy as np
import jax
import jax.numpy as jnp
from jax import lax
from jax.experimental import pallas as pl
from jax.experimental.pallas import tpu as pltpu

F32 = jnp.float32
BF16 = jnp.bfloat16
I32 = jnp.int32

HEAD_DIM = 64
GRID_W = 64
ROPE_THETA = 10000.0
ROPE_PAIRS = HEAD_DIM // 4
N_GROUPS = 4
NORM_EPS = 1e-6
HY_MIN_DECAY = math.log(1e-2) / 1.5
HY_MAX_DECAY = math.log(1e-2) / 0.3

LANES = 128
MOD_ROWS = 8
VMEM_LIMIT_BYTES = 50 * 1024 * 1024

HIGHEST = lax.Precision.HIGHEST
NT_DIMS = (((1,), (1,)), ((), ()))


def _cparams(n_axes):
    return pltpu.CompilerParams(dimension_semantics=("arbitrary",) * n_axes,
                                vmem_limit_bytes=VMEM_LIMIT_BYTES)


def _sigmoid(x):
    return 1.0 / (1.0 + jnp.exp(-x))


def _norm_mod(x, g, mod_ref, shift_row):
    ms = jnp.mean(x * x, axis=-1, keepdims=True)
    y = x * lax.rsqrt(ms + NORM_EPS) * g
    shift = mod_ref[0, shift_row:shift_row + 1, :]
    scale = mod_ref[0, shift_row + 1:shift_row + 2, :]
    return y * (1.0 + scale) + shift


def _norm_mod_kernel(seg_ref, x_ref, g_ref, mod_ref, u_ref, *, shift_row):
    del seg_ref
    u_ref[...] = _norm_mod(x_ref[...], g_ref[...], mod_ref, shift_row).astype(u_ref.dtype)


def norm_mod(x, g, mod, seg, *, shift_row, tm):
    n, d = x.shape
    grid_spec = pltpu.PrefetchScalarGridSpec(
        num_scalar_prefetch=1, grid=(n // tm,),
        in_specs=[pl.BlockSpec((tm, d), lambda i, s: (i, 0)),
                  pl.BlockSpec((1, d), lambda i, s: (0, 0)),
                  pl.BlockSpec((1, MOD_ROWS, d), lambda i, s: (s[i], 0, 0))],
        out_specs=pl.BlockSpec((tm, d), lambda i, s: (i, 0)))
    return pl.pallas_call(
        functools.partial(_norm_mod_kernel, shift_row=shift_row),
        grid_spec=grid_spec, out_shape=jax.ShapeDtypeStruct((n, d), BF16),
        compiler_params=_cparams(1), name="norm_mod")(seg, x, g, mod)


def _first_index(vals, m):
    idx = jnp.full(vals[0].shape, len(vals) - 1, I32)
    for k in reversed(range(len(vals) - 1)):
        idx = jnp.where(vals[k] >= m, k, idx)
    return idx


def _pick(vals, idx):
    out = vals[-1]
    for k in reversed(range(len(vals) - 1)):
        out = jnp.where(idx == k, vals[k], out)
    return out


def _max_list(vals):
    m = vals[0]
    for v in vals[1:]:
        m = jnp.maximum(m, v)
    return m


def _moe_pre_kernel(seg_ref, x_ref, g_ref, mod_ref, rwt_ref, rb_ref, tri_ref,
                    u_ref, e_ref, w_ref, r_ref, cnt_ref, base_ref, *, shift_row, n_groups):
    del seg_ref

    @pl.when(pl.program_id(0) == 0)
    def _():
        base_ref[...] = jnp.zeros(base_ref.shape, F32)

    u = _norm_mod(x_ref[...], g_ref[...], mod_ref, shift_row)
    u_ref[...] = u
    logits = lax.dot_general(rwt_ref[...], u, NT_DIMS, precision=HIGHEST,
                             preferred_element_type=F32)
    n_exp, tm = logits.shape
    epg = n_exp // n_groups
    ex = jnp.exp(logits - jnp.max(logits, axis=0, keepdims=True))
    probs = ex / jnp.sum(ex, axis=0, keepdims=True)
    sel = probs + rb_ref[...]
    sel_rows = [sel[e:e + 1, :] for e in range(n_exp)]
    p_rows = [probs[e:e + 1, :] for e in range(n_exp)]
    scores, i1s, i2s, p1s, p2s = [], [], [], [], []
    for gi in range(n_groups):
        v = sel_rows[gi * epg:(gi + 1) * epg]
        p = p_rows[gi * epg:(gi + 1) * epg]
        m1 = _max_list(v)
        i1 = _first_index(v, m1)
        v2 = [jnp.where(i1 == k, -jnp.inf, v[k]) for k in range(epg)]
        m2 = _max_list(v2)
        i2 = _first_index(v2, m2)
        scores.append(m1 + m2)
        i1s.append(i1)
        i2s.append(i2)
        p1s.append(_pick(p, i1))
        p2s.append(_pick(p, i2))
    gstar = _first_index(scores, _max_list(scores))
    e1 = gstar * epg + _pick(i1s, gstar)
    e2 = gstar * epg + _pick(i2s, gstar)
    p1 = _pick(p1s, gstar)
    p2 = _pick(p2s, gstar)
    tot = p1 + p2
    e_ref[0:1, :] = e1
    e_ref[1:2, :] = e2
    e_ref[2:, :] = jnp.zeros((MOD_ROWS - 2, tm), I32)
    w_ref[0:1, :] = p1 / tot
    w_ref[1:2, :] = p2 / tot
    w_ref[2:, :] = jnp.zeros((MOD_ROWS - 2, tm), F32)

    expert_id = lax.broadcasted_iota(I32, (n_exp, tm), 0)
    oh1 = (expert_id == e1).astype(F32)
    oh2 = (expert_id == e2).astype(F32)
    tri = tri_ref[...]
    cs1 = jnp.dot(oh1.astype(BF16), tri, preferred_element_type=F32)
    cs2 = jnp.dot(oh2.astype(BF16), tri, preferred_element_type=F32)
    base = base_ref[:, 0:1]
    c1 = cs1[:, tm - 1:tm]
    c2 = cs2[:, tm - 1:tm]
    r1 = jnp.sum(oh1 * (base + cs1), axis=0, keepdims=True) - 1.0
    r2 = jnp.sum(oh2 * (base + c1 + cs2), axis=0, keepdims=True) - 1.0
    r_ref[0:1, :] = r1.astype(I32)
    r_ref[1:2, :] = r2.astype(I32)
    r_ref[2:, :] = jnp.zeros((MOD_ROWS - 2, tm), I32)
    total = base + c1 + c2
    base_ref[...] = jnp.broadcast_to(total, base_ref.shape)
    cnt_ref[...] = jnp.broadcast_to(total, cnt_ref.shape)


def moe_pre(x, g, mod, seg, router_wt, router_b_col, tri, *, shift_row, tm):
    n, d = x.shape
    n_exp = router_wt.shape[0]
    grid_spec = pltpu.PrefetchScalarGridSpec(
        num_scalar_prefetch=1, grid=(n // tm,),
        in_specs=[pl.BlockSpec((tm, d), lambda i, s: (i, 0)),
                  pl.BlockSpec((1, d), lambda i, s: (0, 0)),
                  pl.BlockSpec((1, MOD_ROWS, d), lambda i, s: (s[i], 0, 0)),
                  pl.BlockSpec((n_exp, d), lambda i, s: (0, 0)),
                  pl.BlockSpec((n_exp, 1), lambda i, s: (0, 0)),
                  pl.BlockSpec((tm, tm), lambda i, s: (0, 0))],
        out_specs=[pl.BlockSpec((tm, d), lambda i, s: (i, 0)),
                   pl.BlockSpec((MOD_ROWS, tm), lambda i, s: (0, i)),
                   pl.BlockSpec((MOD_ROWS, tm), lambda i, s: (0, i)),
                   pl.BlockSpec((MOD_ROWS, tm), lambda i, s: (0, i)),
                   pl.BlockSpec((n_exp, LANES), lambda i, s: (0, 0))],
        scratch_shapes=[pltpu.VMEM((n_exp, LANES), F32)])
    return pl.pallas_call(
        functools.partial(_moe_pre_kernel, shift_row=shift_row, n_groups=N_GROUPS),
        grid_spec=grid_spec,
        out_shape=[jax.ShapeDtypeStruct((n, d), F32),
                   jax.ShapeDtypeStruct((MOD_ROWS, n), I32),
                   jax.ShapeDtypeStruct((MOD_ROWS, n), F32),
                   jax.ShapeDtypeStruct((MOD_ROWS, n), I32),
                   jax.ShapeDtypeStruct((n_exp, LANES), F32)],
        compiler_params=_cparams(1), name="moe_pre")(seg, x, g, mod, router_wt, router_b_col, tri)


def _linear_kernel(*refs, silu, has_bias, gate_row):
    refs = list(refs)
    refs.pop(0)
    a_ref, w_ref = refs.pop(0), refs.pop(0)
    bias_ref = refs.pop(0) if has_bias else None
    x_ref, mod_ref = (refs.pop(0), refs.pop(0)) if gate_row is not None else (None, None)
    o_ref, wb_ref = refs

    @pl.when(pl.program_id(1) == 0)
    def _():
        wb_ref[...] = w_ref[...].astype(BF16)

    a = a_ref[...]
    if silu:
        a = a * _sigmoid(a)
    acc = jnp.dot(a.astype(BF16), wb_ref[...], preferred_element_type=F32)
    if has_bias:
        acc = acc + bias_ref[...]
    if gate_row is not None:
        acc = x_ref[...] + mod_ref[0, gate_row:gate_row + 1, :] * acc
    o_ref[...] = acc.astype(o_ref.dtype)


def linear(a, w, seg, *, layer, bias=None, resid=None, mod=None, gate_row=None, silu=False,
           tm, tn):
    m, k = a.shape
    n = w.shape[2]
    in_specs = [pl.BlockSpec((tm, k), lambda j, i, s: (i, 0)),
                pl.BlockSpec((None, k, tn), lambda j, i, s: (layer, 0, j))]
    args = [a, w]
    if bias is not None:
        in_specs.append(pl.BlockSpec((None, 1, tn), lambda j, i, s: (layer, 0, j)))
        args.append(bias)
    if gate_row is not None:
        in_specs.append(pl.BlockSpec((tm, tn), lambda j, i, s: (i, j)))
        in_specs.append(pl.BlockSpec((1, MOD_ROWS, tn), lambda j, i, s: (s[i], 0, j)))
        args += [resid, mod]
    grid_spec = pltpu.PrefetchScalarGridSpec(
        num_scalar_prefetch=1, grid=(n // tn, m // tm), in_specs=in_specs,
        out_specs=pl.BlockSpec((tm, tn), lambda j, i, s: (i, j)),
        scratch_shapes=[pltpu.VMEM((k, tn), BF16)])
    return pl.pallas_call(
        functools.partial(_linear_kernel, silu=silu, has_bias=bias is not None, gate_row=gate_row),
        grid_spec=grid_spec, out_shape=jax.ShapeDtypeStruct((m, n), F32),
        compiler_params=_cparams(2), name="linear")(seg, *args)


def _qkv_kernel(*refs, rope, keep_f32, n_a, n_bq, scale):
    refs = list(refs)
    a_ref, w_ref, bd_ref, gains_ref = refs[:4]
    refs = refs[4:]
    if rope:
        c_ref, s_ref = refs[:2]
        refs = refs[2:]
    q_ref, k_ref, v_ref = refs[:3]
    refs = refs[3:]
    if keep_f32:
        kf_ref, vf_ref = refs[:2]
        refs = refs[2:]
    wb_ref, = refs

    @pl.when(pl.program_id(0) == 0)
    def _():
        wb_ref[...] = w_ref[...].astype(BF16)

    acc = jnp.dot(a_ref[...], wb_ref[...], preferred_element_type=F32)
    tm = acc.shape[0]
    bd = bd_ref[...]
    lane = lax.broadcasted_iota(I32, (tm, LANES), 1)
    low_half = lane < HEAD_DIM

    def headnorm(x, row):
        ms = jnp.dot((x * x).astype(BF16), bd, preferred_element_type=F32)
        return x * lax.rsqrt(ms + NORM_EPS) * gains_ref[row:row + 1, :]

    def rot(x):
        if not rope:
            return x
        partner = jnp.where((lane & ROPE_PAIRS) == 0,
                            pltpu.roll(x, LANES - ROPE_PAIRS, 1), pltpu.roll(x, ROPE_PAIRS, 1))
        return x * c_ref[...] + partner * s_ref[...]

    def chunk(col):
        return acc[:, col * LANES:(col + 1) * LANES]

    def dup(x):
        swapped = pltpu.roll(x, HEAD_DIM, 1)
        return jnp.where(low_half, x, swapped), jnp.where(low_half, swapped, x)

    for c in range(n_a):
        q_ref[:, c * LANES:(c + 1) * LANES] = (rot(headnorm(chunk(c), 0)) * scale).astype(BF16)
        kn = headnorm(chunk(n_a + c), 1)
        k_ref[:, c * LANES:(c + 1) * LANES] = rot(kn).astype(BF16)
        vv = chunk(2 * n_a + c)
        v_ref[:, c * LANES:(c + 1) * LANES] = vv.astype(BF16)
        if keep_f32:
            kf_ref[:, c * LANES:(c + 1) * LANES] = kn
            vf_ref[:, c * LANES:(c + 1) * LANES] = vv
    for c in range(n_bq):
        q_ref[:, (n_a + c) * LANES:(n_a + c + 1) * LANES] = (
            rot(headnorm(chunk(3 * n_a + c), 2)) * scale).astype(BF16)
    kb = headnorm(chunk(3 * n_a + n_bq), 3)
    vb = chunk(3 * n_a + n_bq + 1)
    k0, k1 = dup(rot(kb))
    v0, v1 = dup(vb)
    k_ref[:, n_a * LANES:(n_a + 1) * LANES] = k0.astype(BF16)
    k_ref[:, (n_a + 1) * LANES:(n_a + 2) * LANES] = k1.astype(BF16)
    v_ref[:, n_a * LANES:(n_a + 1) * LANES] = v0.astype(BF16)
    v_ref[:, (n_a + 1) * LANES:(n_a + 2) * LANES] = v1.astype(BF16)
    if keep_f32:
        kf_ref[:, n_a * LANES:(n_a + 1) * LANES] = kb
        vf_ref[:, n_a * LANES:(n_a + 1) * LANES] = vb


def qkv_project(u, w, bd, gains, rope_cs, *, layer, row0, m, keep_f32, n_a, n_bq, tm):
    d = u.shape[1]
    n = w.shape[2]
    assert n == (3 * n_a + n_bq + 2) * LANES and row0 % tm == 0
    rope = rope_cs is not None
    tile0 = row0 // tm
    in_specs = [pl.BlockSpec((tm, d), lambda i: (tile0 + i, 0)),
                pl.BlockSpec((None, d, n), lambda i: (layer, 0, 0)),
                pl.BlockSpec((LANES, LANES), lambda i: (0, 0)),
                pl.BlockSpec((MOD_ROWS, LANES), lambda i: (0, 0))]
    args = [u, w, bd, gains]
    if rope:
        rows = rope_cs[0].shape[0] // tm
        in_specs += [pl.BlockSpec((tm, LANES), lambda i: (i % rows, 0))] * 2
        args += list(rope_cs)
    qw, kw = (n_a + n_bq) * LANES, (n_a + 2) * LANES
    out_specs = [pl.BlockSpec((tm, qw), lambda i: (i, 0)),
                 pl.BlockSpec((tm, kw), lambda i: (i, 0)),
                 pl.BlockSpec((tm, kw), lambda i: (i, 0))]
    out_shape = [jax.ShapeDtypeStruct((m, qw), BF16), jax.ShapeDtypeStruct((m, kw), BF16),
                 jax.ShapeDtypeStruct((m, kw), BF16)]
    if keep_f32:
        fw = (n_a + 1) * LANES
        out_specs += [pl.BlockSpec((tm, fw), lambda i: (i, 0))] * 2
        out_shape += [jax.ShapeDtypeStruct((m, fw), F32)] * 2
    return pl.pallas_call(
        functools.partial(_qkv_kernel, rope=rope, keep_f32=keep_f32, n_a=n_a, n_bq=n_bq,
                          scale=HEAD_DIM ** -0.5 * math.log2(math.e)),
        grid=(m // tm,), in_specs=in_specs, out_specs=out_specs, out_shape=out_shape,
        scratch_shapes=[pltpu.VMEM((d, n), BF16)],
        compiler_params=_cparams(1), name="qkv")(*args)


def _attn_kernel(*refs, n_diff, out_scale, tk, n_chunks, n_qt, has_cache, has_init):
    refs = list(refs)
    q_ref = refs.pop(0)
    k_refs, v_refs = [], []
    for _ in range(2 if has_cache else 1):
        k_refs.append(refs.pop(0))
        v_refs.append(refs.pop(0))
    lam_ref, sg_ref = refs.pop(0), refs.pop(0)
    if has_init:
        refs.pop(0)
    o_ref, sa_ref, sb_ref, ma_ref, mb_ref = refs

    t = pl.program_id(0)
    tq = q_ref.shape[0]
    blocks = [(src, j) for src in range(len(k_refs)) for j in range(k_refs[src].shape[0] // tk)]
    lane = lax.broadcasted_iota(I32, (tq, LANES), 1)
    low_half = lane < HEAD_DIM
    c_prev = (jnp.maximum(t - 1, 0) // n_qt) % n_chunks

    @pl.when(t == 0)
    def _():
        sb_ref[...] = jnp.zeros(sb_ref.shape, F32)
        mb_ref[...] = jnp.zeros(mb_ref.shape, F32)

    def step(s_cur, m_cur, s_prev, m_prev):
        qf = q_ref[...].astype(F32)
        q01 = jnp.concatenate([jnp.where(low_half, qf, 0.0), jnp.where(low_half, 0.0, qf)],
                              axis=0).astype(BF16)
        mx = None
        for n, (src, j) in enumerate(blocks):
            s = lax.dot_general(q01, k_refs[src][j * tk:(j + 1) * tk, :], NT_DIMS,
                                preferred_element_type=F32)
            s_cur[:, n * tk:(n + 1) * tk] = s
            for u in range(tk // LANES):
                blk = s[:, u * LANES:(u + 1) * LANES]
                mx = blk if mx is None else jnp.maximum(mx, blk)
        m_cur[...] = jnp.max(mx, axis=-1, keepdims=True)

        m = m_prev[...]
        lsum = jnp.zeros((2 * tq, LANES), F32)
        acc = jnp.zeros((2 * tq, LANES), F32)
        for n, (src, j) in enumerate(blocks):
            p = jnp.exp2(s_prev[:, n * tk:(n + 1) * tk] - m)
            for u in range(tk // LANES):
                lsum = lsum + p[:, u * LANES:(u + 1) * LANES]
            acc = acc + jnp.dot(p.astype(BF16), v_refs[src][j * tk:(j + 1) * tk, :],
                                preferred_element_type=F32)
        o = acc / jnp.sum(lsum, axis=-1, keepdims=True)
        o0, o1 = o[:tq], o[tq:]
        od = o0 - lam_ref[...] * o1
        ms = jnp.mean(od * od, axis=-1, keepdims=True)
        od = od * lax.rsqrt(ms + NORM_EPS) * sg_ref[...] * out_scale
        og = jnp.where(low_half, o0, o1)
        o_ref[...] = jnp.where(c_prev < n_diff, od, og).astype(o_ref.dtype)

    @pl.when(t % 2 == 0)
    def _():
        step(sa_ref, ma_ref, sb_ref, mb_ref)

    @pl.when(t % 2 == 1)
    def _():
        step(sb_ref, mb_ref, sa_ref, ma_ref)


def attention(q, k_new, v_new, cache_kv, lam_row, subln_row, o_init, *, batch, row0, n_out,
              n_diff, out_scale, tq, tk):
    m, qw = q.shape
    lq = m // batch
    n_chunks = qw // LANES
    n_qt = lq // tq
    n_tiles = batch * n_chunks * n_qt
    tile0 = row0 // tq
    assert row0 % tq == 0
    kv = ([] if cache_kv is None else list(cache_kv)) + [k_new, v_new]
    s_total = sum(a.shape[1] for a in kv[0::2])

    def tile(n):
        return n // (n_chunks * n_qt), (n // n_qt) % n_chunks, n % n_qt

    def kv_chunk(c):
        return jnp.where(c < n_diff, c, n_diff + (c - n_diff) // 2)

    def q_map(t):
        bi, c, i = tile(jnp.minimum(t, n_tiles - 1))
        return bi * n_qt + i, c

    def k_map(t):
        bi, c, _ = tile(jnp.minimum(t, n_tiles - 1))
        return bi, 0, kv_chunk(c)

    def v_map(t):
        bi, c, _ = tile(jnp.maximum(t - 1, 0))
        return bi, 0, kv_chunk(c)

    def o_map(t):
        bi, c, i = tile(jnp.maximum(t - 1, 0))
        return tile0 + bi * n_qt + i, c

    in_specs = [pl.BlockSpec((tq, LANES), q_map)]
    for a in kv[0::2]:
        in_specs += [pl.BlockSpec((None, a.shape[1], LANES), k_map),
                     pl.BlockSpec((None, a.shape[1], LANES), v_map)]
    in_specs += [pl.BlockSpec((1, LANES), lambda t: (0, 0))] * 2
    args = [q] + kv + [lam_row, subln_row]
    aliases = {}
    if o_init is not None:
        in_specs.append(pl.BlockSpec(memory_space=pl.ANY))
        aliases = {len(args): 0}
        args.append(o_init)
    return pl.pallas_call(
        functools.partial(_attn_kernel, n_diff=n_diff, out_scale=out_scale, tk=tk,
                          n_chunks=n_chunks, n_qt=n_qt, has_cache=cache_kv is not None,
                          has_init=o_init is not None),
        grid=(n_tiles + 1,), in_specs=in_specs,
        out_specs=pl.BlockSpec((tq, LANES), o_map),
        out_shape=jax.ShapeDtypeStruct((n_out, qw), BF16),
        input_output_aliases=aliases,
        scratch_shapes=[pltpu.VMEM((2 * tq, s_total), F32), pltpu.VMEM((2 * tq, s_total), F32),
                        pltpu.VMEM((2 * tq, 1), F32), pltpu.VMEM((2 * tq, 1), F32)],
        compiler_params=_cparams(1), name="attention")(*args)


DMA_UNROLL = 8


def _row_copy(src_ref, row, dst_ref, dst_row, sem):
    return pltpu.make_async_copy(src_ref.at[pl.ds(row, 1), :], dst_ref.at[pl.ds(dst_row, 1), :], sem)


def _dispatch_kernel(slot_ref, h_ref, xs_in_hbm, xs_hbm, sem):
    del xs_in_hbm
    rows = h_ref.shape[0]

    def start(r, carry):
        for kk in range(2):
            _row_copy(h_ref, r, xs_hbm, slot_ref[0, 0, kk * rows + r], sem).start()
        return carry

    def wait(r, carry):
        for kk in range(2):
            _row_copy(h_ref, r, xs_hbm, 0, sem).wait()
        return carry

    lax.fori_loop(0, rows, start, 0, unroll=DMA_UNROLL)
    lax.fori_loop(0, rows, wait, 0, unroll=DMA_UNROLL)


def moe_dispatch(h, slots, xs_zero, *, td):
    n, d = h.shape
    tiles = n // td
    slot_tiles = slots.reshape(2, tiles, td).transpose(1, 0, 2).reshape(tiles, 1, 2 * td)
    return pl.pallas_call(
        _dispatch_kernel, grid=(tiles,),
        in_specs=[pl.BlockSpec((1, 1, 2 * td), lambda i: (i, 0, 0), memory_space=pltpu.SMEM),
                  pl.BlockSpec((td, d), lambda i: (i, 0)),
                  pl.BlockSpec(memory_space=pl.ANY)],
        out_specs=pl.BlockSpec(memory_space=pl.ANY),
        out_shape=jax.ShapeDtypeStruct(xs_zero.shape, xs_zero.dtype),
        input_output_aliases={2: 0},
        scratch_shapes=[pltpu.SemaphoreType.DMA(())],
        compiler_params=_cparams(1), name="moe_dispatch")(slot_tiles, h, xs_zero)


def _ffn_kernel(te_ref, tv_ref, x_ref, wg_ref, wu_ref, wd_ref, o_ref, wgb_ref, wub_ref, wdb_ref):
    t = pl.program_id(0)
    prev = te_ref[jnp.maximum(t - 1, 0)]
    new_expert = jnp.logical_or(t == 0, te_ref[t] != prev)

    @pl.when(new_expert)
    def _():
        wgb_ref[...] = wg_ref[...].astype(BF16)
        wub_ref[...] = wu_ref[...].astype(BF16)
        wdb_ref[...] = wd_ref[...].astype(BF16)

    @pl.when(tv_ref[t] != 0)
    def _():
        x = x_ref[...].astype(BF16)
        gate = jnp.dot(x, wgb_ref[...], preferred_element_type=F32)
        up = jnp.dot(x, wub_ref[...], preferred_element_type=F32)
        h = (gate * _sigmoid(gate) * up).astype(BF16)
        o_ref[...] = jnp.dot(h, wdb_ref[...], preferred_element_type=F32)

    @pl.when(tv_ref[t] == 0)
    def _():
        o_ref[...] = jnp.zeros(o_ref.shape, o_ref.dtype)


def moe_ffn(xs, tile_expert, tile_valid, wg, wu, wd, *, layer, tm):
    p, d = xs.shape
    f = wg.shape[3]
    grid_spec = pltpu.PrefetchScalarGridSpec(
        num_scalar_prefetch=2, grid=(p // tm,),
        in_specs=[pl.BlockSpec((tm, d), lambda t, te, tv: (t, 0)),
                  pl.BlockSpec((None, None, d, f), lambda t, te, tv: (layer, te[t], 0, 0)),
                  pl.BlockSpec((None, None, d, f), lambda t, te, tv: (layer, te[t], 0, 0)),
                  pl.BlockSpec((None, None, f, d), lambda t, te, tv: (layer, te[t], 0, 0))],
        out_specs=pl.BlockSpec((tm, d), lambda t, te, tv: (t, 0)),
        scratch_shapes=[pltpu.VMEM((d, f), BF16), pltpu.VMEM((d, f), BF16),
                        pltpu.VMEM((f, d), BF16)])
    return pl.pallas_call(
        _ffn_kernel, grid_spec=grid_spec, out_shape=jax.ShapeDtypeStruct((p, d), F32),
        compiler_params=_cparams(1), name="moe_ffn")(tile_expert, tile_valid, xs, wg, wu, wd)


def _combine_kernel(seg_ref, slot_ref, y_hbm, x_ref, w0_ref, w1_ref, mod_ref, o_ref, buf_ref, sem,
                    *, gate_row):
    del seg_ref
    rows = x_ref.shape[0]

    def start(r, carry):
        for kk in range(2):
            _row_copy(y_hbm, slot_ref[0, 0, kk * rows + r], buf_ref.at[kk], r, sem).start()
        return carry

    def wait(r, carry):
        for kk in range(2):
            _row_copy(y_hbm, 0, buf_ref.at[kk], r, sem).wait()
        return carry

    lax.fori_loop(0, rows, start, 0, unroll=DMA_UNROLL)
    lax.fori_loop(0, rows, wait, 0, unroll=DMA_UNROLL)
    moe = w0_ref[...] * buf_ref[0] + w1_ref[...] * buf_ref[1]
    o_ref[...] = x_ref[...] + mod_ref[0, gate_row:gate_row + 1, :] * moe


def moe_combine(y, slots, x, w0, w1, mod, seg, *, gate_row, tc, row0=0, rows=None):
    n, d = x.shape
    rows = n if rows is None else rows
    tiles = n // tc
    tile0 = row0 // tc
    assert row0 % tc == 0 and rows % tc == 0
    slot_tiles = slots.reshape(2, tiles, tc).transpose(1, 0, 2).reshape(tiles, 1, 2 * tc)
    grid_spec = pltpu.PrefetchScalarGridSpec(
        num_scalar_prefetch=1, grid=(rows // tc,),
        in_specs=[pl.BlockSpec((1, 1, 2 * tc), lambda i, s: (tile0 + i, 0, 0),
                               memory_space=pltpu.SMEM),
                  pl.BlockSpec(memory_space=pl.ANY),
                  pl.BlockSpec((tc, d), lambda i, s: (tile0 + i, 0)),
                  pl.BlockSpec((tc, 1), lambda i, s: (tile0 + i, 0)),
                  pl.BlockSpec((tc, 1), lambda i, s: (tile0 + i, 0)),
                  pl.BlockSpec((1, MOD_ROWS, d), lambda i, s: (s[tile0 + i], 0, 0))],
        out_specs=pl.BlockSpec((tc, d), lambda i, s: (i, 0)),
        scratch_shapes=[pltpu.VMEM((2, tc, d), F32), pltpu.SemaphoreType.DMA(())])
    return pl.pallas_call(
        functools.partial(_combine_kernel, gate_row=gate_row),
        grid_spec=grid_spec, out_shape=jax.ShapeDtypeStruct((rows, d), F32),
        compiler_params=_cparams(1), name="moe_combine")(seg, slot_tiles, y, x, w0, w1, mod)


def _route(e_idx, rank, counts, *, tm, p_max):
    n_exp = counts.shape[0]
    padded = ((counts + tm - 1) // tm) * tm
    ends = jnp.cumsum(padded)
    starts = ends - padded
    expert = jnp.arange(n_exp, dtype=I32)
    slots = jnp.sum(jnp.where(e_idx[..., None] == expert, starts, 0), axis=-1) + rank
    tile_start = jnp.arange(p_max // tm, dtype=I32) * tm
    tile_expert = jnp.minimum(jnp.sum((tile_start[:, None] >= ends[None, :]).astype(I32), axis=1),
                              n_exp - 1)
    tile_valid = (tile_start < ends[-1]).astype(I32)
    return slots, tile_expert, tile_valid


def _linear_nt_kernel(wt_ref, u_ref, bias_ref, o_ref, wb_ref):
    @pl.when(jnp.logical_and(pl.program_id(1) == 0, pl.program_id(2) == 0))
    def _():
        wb_ref[...] = wt_ref[...].astype(BF16)

    acc = lax.dot_general(wb_ref[...], u_ref[...], NT_DIMS, preferred_element_type=F32)
    o_ref[...] = acc + bias_ref[...]


def linear_nt(wt, u, bias_col, *, row0, batch, l, tn, tl):
    n, k = wt.shape
    b = batch
    lt = l // tl
    tile0 = row0 // tl
    assert row0 % tl == 0
    return pl.pallas_call(
        _linear_nt_kernel, grid=(n // tn, b, lt),
        in_specs=[pl.BlockSpec((tn, k), lambda j, bi, i: (j, 0)),
                  pl.BlockSpec((tl, k), lambda j, bi, i: (tile0 + bi * lt + i, 0)),
                  pl.BlockSpec((tn, 1), lambda j, bi, i: (j, 0))],
        out_specs=pl.BlockSpec((None, tn, tl), lambda j, bi, i: (bi, j, i)),
        out_shape=jax.ShapeDtypeStruct((b, n, l), F32),
        scratch_shapes=[pltpu.VMEM((tn, k), BF16)],
        compiler_params=_cparams(3), name="linear_nt")(wt, u, bias_col)


def _filter_mlp_kernel(z_ref, w1_ref, b1_ref, f1_ref, w2_ref, b2_ref, f2_ref, o_ref):
    h = jnp.dot(w1_ref[...], z_ref[...], precision=HIGHEST, preferred_element_type=F32)
    h = jnp.sin(f1_ref[...] * (h + b1_ref[...]))
    h = jnp.dot(w2_ref[...], h, precision=HIGHEST, preferred_element_type=F32)
    o_ref[...] = jnp.sin(f2_ref[...] * (h + b2_ref[...]))


def filter_mlp(zt, w1t, b1, f1, w2t, b2, f2):
    width, l = w2t.shape[0], zt.shape[1]
    return pl.pallas_call(
        _filter_mlp_kernel, out_shape=jax.ShapeDtypeStruct((width, l), F32),
        compiler_params=pltpu.CompilerParams(vmem_limit_bytes=VMEM_LIMIT_BYTES),
        name="hyena_filter_mlp")(zt, w1t, b1, f1, w2t, b2, f2)


def _filter_out_kernel(wf_ref, wb_ref, h_ref, t_ref, delta_ref, o_ref):
    l = h_ref.shape[1] // 2
    window = jnp.exp(-t_ref[...] * delta_ref[...])
    h = h_ref[...].astype(BF16)
    neg = jnp.dot(wb_ref[...].astype(BF16), h[:, :l], preferred_element_type=F32) * window[:, :l]
    pos = jnp.dot(wf_ref[...].astype(BF16), h[:, l:], preferred_element_type=F32) * window[:, l:]
    lane = lax.broadcasted_iota(I32, neg.shape, 1)
    o_ref[:, :l] = jnp.where(lane == 0, 0.0, neg)
    o_ref[:, l:] = pos


def filter_out(w3t, h2t, t_row, delta_col, *, n_ord, tr):
    width = w3t.shape[1]
    two_l = h2t.shape[1]
    d = delta_col.shape[0]
    dt = d // tr
    return pl.pallas_call(
        _filter_out_kernel, grid=(n_ord, dt),
        in_specs=[pl.BlockSpec((tr, width), lambda o, i: ((2 * o) * dt + i, 0)),
                  pl.BlockSpec((tr, width), lambda o, i: ((2 * o + 1) * dt + i, 0)),
                  pl.BlockSpec((width, two_l), lambda o, i: (0, 0)),
                  pl.BlockSpec((1, two_l), lambda o, i: (0, 0)),
                  pl.BlockSpec((tr, 1), lambda o, i: (i, 0))],
        out_specs=pl.BlockSpec((None, tr, two_l), lambda o, i: (o, i, 0)),
        out_shape=jax.ShapeDtypeStruct((n_ord, d, two_l), F32),
        compiler_params=_cparams(2), name="hyena_filter_out")(w3t, w3t, h2t, t_row, delta_col)


def _spectrum_kernel(a_ref, b_ref, wf_ref, sgn_ref, o_ref):
    a = a_ref[...]
    lane = lax.broadcasted_iota(I32, a.shape, 1)
    a = jnp.where(lane == 0, 0.0, a)
    wf = wf_ref[...]
    ta = jnp.dot(a.astype(BF16), wf, preferred_element_type=F32)
    tb = jnp.dot(b_ref[...].astype(BF16), wf, preferred_element_type=F32)
    o_ref[...] = sgn_ref[...] * ta + tb


def filter_spectra(f_full, wf, sgn, *, p, tr):
    n_ord, d, two_l = f_full.shape
    nseg = two_l // p - 1
    return pl.pallas_call(
        _spectrum_kernel, grid=(n_ord, d // tr, nseg),
        in_specs=[pl.BlockSpec((None, tr, p), lambda o, i, j: (o, i, j)),
                  pl.BlockSpec((None, tr, p), lambda o, i, j: (o, i, j + 1)),
                  pl.BlockSpec((p, 2 * p), lambda o, i, j: (0, 0)),
                  pl.BlockSpec((1, 2 * p), lambda o, i, j: (0, 0))],
        out_specs=pl.BlockSpec((None, None, tr, 2 * p), lambda o, i, j: (o, j, i, 0)),
        out_shape=jax.ShapeDtypeStruct((n_ord, nseg, d, 2 * p), F32),
        compiler_params=_cparams(3), name="hyena_filter_spectra")(f_full, f_full, wf, sgn)


SUBLANES = 8
MAC_OUT_BLOCKS = 4


def _hyena_conv_kernel(pv_ref, p1_ref, p2_ref, cv_ref, c1_ref, c2_ref, sk_ref, g_ref,
                       wf_ref, wi_ref, o_ref, zh_ref, yh_ref, *, p, nb):
    td, l = pv_ref.shape
    lane_l = lax.broadcasted_iota(I32, (td, l), 1)
    dc = lax.broadcasted_iota(I32, (SUBLANES, LANES), 1) == 0

    def short_conv(x_ref, cw_ref):
        x = x_ref[...]
        cw = cw_ref[...]
        prev = jnp.where(lane_l == 0, 0.0, pltpu.roll(x, 1, 1))
        nxt = jnp.where(lane_l == l - 1, 0.0, pltpu.roll(x, l - 1, 1))
        return prev * cw[:, 0:1] + x * cw[:, 1:2] + nxt * cw[:, 2:3] + cw[:, 3:4]

    def spectral_mac(order):
        def rows_body(r, carry):
            r0 = pl.multiple_of(r * SUBLANES, SUBLANES)

            def zrows(j):
                return pl.ds(pl.multiple_of(j * td + r0, SUBLANES), SUBLANES)

            for c in range(p // LANES):
                re = slice(c * LANES, (c + 1) * LANES)
                im = slice(p + c * LANES, p + (c + 1) * LANES)
                zr = [zh_ref[zrows(j), re] for j in range(nb)]
                zi = [zh_ref[zrows(j), im] for j in range(nb)]
                for i0 in range(0, nb, MAC_OUT_BLOCKS):
                    outs = range(i0, min(i0 + MAC_OUT_BLOCKS, nb))
                    lags = sorted({i - j + nb - 1 for i in outs for j in range(nb)})
                    gr = {m: g_ref[order, m, pl.ds(r0, SUBLANES), re] for m in lags}
                    gi = {m: g_ref[order, m, pl.ds(r0, SUBLANES), im] for m in lags}
                    for i in outs:
                        a_re = a_im = a_ii = None
                        for j in range(nb):
                            m = i - j + nb - 1
                            if c == 0:
                                t_re, t_ii = zr[j] * gr[m], zi[j] * gi[m]
                                a_ii = t_ii if a_ii is None else a_ii + t_ii
                            else:
                                t_re = zr[j] * gr[m] - zi[j] * gi[m]
                            t_im = zr[j] * gi[m] + zi[j] * gr[m]
                            a_re = t_re if a_re is None else a_re + t_re
                            a_im = t_im if a_im is None else a_im + t_im
                        if c == 0:
                            a_re, a_im = a_re - jnp.where(dc, 0.0, a_ii), jnp.where(dc, a_ii, a_im)
                        yh_ref[zrows(i), re] = a_re
                        yh_ref[zrows(i), im] = a_im
            return carry

        lax.fori_loop(0, td // SUBLANES, rows_body, 0)

    def long_conv(z, order):
        zb = jnp.concatenate([z[:, j * p:(j + 1) * p] for j in range(nb)], axis=0).astype(BF16)
        zh_ref[...] = jnp.dot(zb, wf_ref[...], preferred_element_type=F32)
        spectral_mac(order)
        y = jnp.dot(yh_ref[...].astype(BF16), wi_ref[...], preferred_element_type=F32)
        return jnp.concatenate([y[i * td:(i + 1) * td] for i in range(nb)], axis=1)

    v = short_conv(pv_ref, cv_ref)
    sk = sk_ref[...]
    z1 = short_conv(p1_ref, c1_ref) * (long_conv(v, 0) + v * sk[:, 0:1])
    o_ref[...] = short_conv(p2_ref, c2_ref) * (long_conv(z1, 1) + z1 * sk[:, 1:2])


def hyena_conv(pt, cwb, sk, g, wf, wi, *, p, td):
    b, d3, l = pt.shape
    d = d3 // 3
    nb = l // p
    n_ord, nseg = g.shape[:2]
    dt = d // td
    return pl.pallas_call(
        functools.partial(_hyena_conv_kernel, p=p, nb=nb),
        grid=(dt, b),
        in_specs=[pl.BlockSpec((None, td, l), lambda i, bi: (bi, i, 0)),
                  pl.BlockSpec((None, td, l), lambda i, bi: (bi, dt + i, 0)),
                  pl.BlockSpec((None, td, l), lambda i, bi: (bi, 2 * dt + i, 0)),
                  pl.BlockSpec((td, MOD_ROWS), lambda i, bi: (i, 0)),
                  pl.BlockSpec((td, MOD_ROWS), lambda i, bi: (dt + i, 0)),
                  pl.BlockSpec((td, MOD_ROWS), lambda i, bi: (2 * dt + i, 0)),
                  pl.BlockSpec((td, MOD_ROWS), lambda i, bi: (i, 0)),
                  pl.BlockSpec((n_ord, nseg, td, 2 * p), lambda i, bi: (0, 0, i, 0)),
                  pl.BlockSpec((p, 2 * p), lambda i, bi: (0, 0)),
                  pl.BlockSpec((2 * p, p), lambda i, bi: (0, 0))],
        out_specs=pl.BlockSpec((None, td, l), lambda i, bi: (bi, i, 0)),
        out_shape=jax.ShapeDtypeStruct((b, d, l), F32),
        scratch_shapes=[pltpu.VMEM((nb * td, 2 * p), F32), pltpu.VMEM((nb * td, 2 * p), F32)],
        compiler_params=_cparams(2), name="hyena_conv")(pt, pt, pt, cwb, cwb, cwb, sk, g, wf, wi)


def _dft_tables(p):
    n = jnp.arange(p, dtype=I32)
    ang = (jnp.pi / p) * ((n[:, None] * n[None, :]) % (2 * p)).astype(F32)
    cos, sin = jnp.cos(ang), jnp.sin(ang)
    alt = jnp.where(n % 2 == 0, 1.0, -1.0).astype(F32)
    is_dc = (n == 0)
    fwd_im = jnp.where(is_dc[None, :], alt[:, None], -sin)
    wf = jnp.concatenate([cos, fwd_im], axis=1)
    inv_re = jnp.where(is_dc[:, None], 0.5 / p, cos / p)
    inv_im = jnp.where(is_dc[:, None], (0.5 / p) * alt[None, :], -sin / p)
    wi = jnp.concatenate([inv_re, inv_im], axis=0)
    sgn = jnp.concatenate([alt, alt])[None, :]
    return wf.astype(BF16), wi.astype(BF16), sgn


def _hyena_filter_spectra(l, p, prm, wf, sgn, d):
    f_w1, f_b1, f_fr1, f_w2, f_b2, f_fr2, f_w3 = prm
    emb = f_w1.shape[0]
    bands_n = (emb - 1) // 2
    pos = jnp.arange(l, dtype=F32)[None, :]
    t = jnp.linspace(0.0, 1.0, l, dtype=F32)[None, :]
    bands = jnp.linspace(1e-4, bands_n - 1, bands_n, dtype=F32)[:, None]
    ang = (2.0 * math.pi / l) * pos * bands
    emb_pad = -(-emb // LANES) * LANES
    zt = jnp.concatenate([t, jnp.cos(ang), -jnp.sin(ang), jnp.zeros((emb_pad - emb, l), F32)], axis=0)
    zt2 = jnp.concatenate([zt[:, :1], jnp.flip(zt[:, 1:], axis=1), zt], axis=1)
    w1t = jnp.pad(f_w1.T, ((0, 0), (0, emb_pad - emb)))
    h2t = filter_mlp(zt2, w1t, f_b1[:, None], f_fr1[:, None], f_w2.T, f_b2[:, None], f_fr2[:, None])
    deltas = jnp.abs(jnp.linspace(HY_MIN_DECAY, HY_MAX_DECAY, d, dtype=F32))[:, None]
    n_ord = f_w3.shape[1] // (2 * d)
    f_full = filter_out(f_w3.T, h2t, zt2[:1], deltas, n_ord=n_ord, tr=128)
    return filter_spectra(f_full, wf, sgn, p=p, tr=256)


def _rope_tables(n_tok):
    rows = n_tok // GRID_W
    row = jnp.repeat(jnp.arange(rows), GRID_W).astype(F32)
    col = jnp.tile(jnp.arange(GRID_W), rows).astype(F32)
    inv = ROPE_THETA ** (-jnp.arange(ROPE_PAIRS, dtype=F32) / ROPE_PAIRS)
    ang_r, ang_c = row[:, None] * inv, col[:, None] * inv
    cos_h = jnp.concatenate([jnp.cos(ang_r)] * 2 + [jnp.cos(ang_c)] * 2, axis=-1)
    sin_h = jnp.concatenate([-jnp.sin(ang_r), jnp.sin(ang_r), -jnp.sin(ang_c), jnp.sin(ang_c)],
                            axis=-1)
    reps = LANES // HEAD_DIM
    return jnp.tile(cos_h, (1, reps)), jnp.tile(sin_h, (1, reps))


def _dup_heads(x):
    h0, h1 = x[..., :HEAD_DIM], x[..., HEAD_DIM:]
    return jnp.concatenate([h0, h0, h1, h1], axis=-1)


def kernel(x_prompt, x_sample, cache_a_k, cache_a_v, cache_b_k, cache_b_v, c, c_ctx,
           router_w, router_b, ada_w, ada_b, norm1_g, norm2_g, w_qkv, w_o,
           qn_a, kn_a, qn_b, kn_b, lam_q1, lam_k1, lam_q2, lam_k2, subln_a,
           hy_w_in, hy_b_in, hy_conv_w, hy_conv_b, hy_f_w1, hy_f_b1, hy_f_freq1,
           hy_f_w2, hy_f_b2, hy_f_freq2, hy_f_w3, hy_bias, hy_w_out, hy_b_out,
           moe_wg, moe_wu, moe_wd):
    nb_p, seq, d = x_prompt.shape
    nb_s, dec_seq, _ = x_sample.shape
    past = cache_a_k.shape[2]
    depth = ada_w.shape[0]
    n_exp = router_w.shape[1]
    n_p, n_s = nb_p * seq, nb_s * dec_seq
    n_tok = n_p + n_s
    n_a = cache_a_v.shape[3]
    n_bq = (w_qkv.shape[2] // LANES) - 3 * n_a - 2
    assert cache_b_k.shape[3] * HEAD_DIM == LANES and n_bq == 2 * (cache_b_k.shape[3])

    tm = 512
    assert n_p % tm == 0 and dec_seq % tm == 0
    seg_rows = np.concatenate([np.zeros(n_p, np.int32),
                               1 + np.arange(n_s, dtype=np.int32) // dec_seq])
    seg = jnp.asarray(seg_rows[::tm])
    tc = 256
    seg_c = jnp.asarray(seg_rows[::tc])
    seg0 = jnp.zeros((1,), I32)

    x = jnp.concatenate([x_prompt.reshape(n_p, d), x_sample.reshape(n_s, d)], axis=0)
    cvec = jnp.concatenate([c_ctx[None, :], c, jnp.zeros((16 - 1 - nb_s, d), F32)], axis=0)
    bd = (jnp.arange(LANES)[:, None] // HEAD_DIM == jnp.arange(LANES)[None, :] // HEAD_DIM)
    bd = (bd.astype(F32) / HEAD_DIM).astype(BF16)
    rope_cs = _rope_tables(dec_seq)
    router_wt = router_w.T
    router_b_col = router_b[:, None]
    tri = (jnp.arange(tm)[:, None] <= jnp.arange(tm)[None, :]).astype(BF16)
    reps = LANES // HEAD_DIM

    new_ak, new_av, new_bk, new_bv = [], [], [], []
    for l in range(depth):
        m = linear(cvec, ada_w, seg0, layer=l, bias=ada_b[:, None, :], silu=True, tm=16, tn=1024)
        mod = jnp.pad(m[:1 + nb_s].reshape(1 + nb_s, 6, d), ((0, 0), (0, MOD_ROWS - 6), (0, 0)))
        u = norm_mod(x, norm1_g[l][None, :], mod, seg, shift_row=0, tm=tm)
        if l % 2 == 0:
            i = l // 2
            lam_init = 0.8 - 0.6 * math.exp(-0.3 * l)
            lam = (jnp.exp(jnp.sum(lam_q1[i] * lam_k1[i])) - jnp.exp(jnp.sum(lam_q2[i] * lam_k2[i]))
                   + lam_init)
            lam_row = jnp.full((1, LANES), lam, F32)
            gains = jnp.concatenate(
                [jnp.tile(gn[i], reps)[None, :] for gn in (qn_a, kn_a, qn_b, kn_b)]
                + [jnp.zeros((MOD_ROWS - 4, LANES), F32)], axis=0)
            sub_row = subln_a[i][None, :]
            qp, kp, vp, kf, vf = qkv_project(u, w_qkv, bd, gains, None, layer=i, row0=0, m=n_p,
                                             keep_f32=True, n_a=n_a, n_bq=n_bq, tm=256)
            aw = n_a * LANES
            new_ak.append(kf[:, :aw].reshape(nb_p, seq, 2 * n_a, HEAD_DIM))
            new_av.append(vf[:, :aw].reshape(nb_p, seq, n_a, 2 * HEAD_DIM))
            new_bk.append(kf[:, aw:].reshape(nb_p, seq, LANES // HEAD_DIM, HEAD_DIM))
            new_bv.append(vf[:, aw:].reshape(nb_p, seq, LANES // HEAD_DIM, HEAD_DIM))
            o = attention(qp, kp.reshape(nb_p, seq, -1), vp.reshape(nb_p, seq, -1), None,
                          lam_row, sub_row, jnp.zeros((n_tok, qp.shape[1]), BF16),
                          batch=nb_p, row0=0, n_out=n_tok,
                          n_diff=n_a, out_scale=1.0 - lam_init, tq=seq, tk=seq)
            qs, ks, vs = qkv_project(u, w_qkv, bd, gains, rope_cs, layer=i, row0=n_p, m=n_s,
                                     keep_f32=False, n_a=n_a, n_bq=n_bq, tm=256)
            ck = jnp.concatenate([cache_a_k[:, i].reshape(nb_s, past, aw),
                                  _dup_heads(cache_b_k[:, i].reshape(nb_s, past, LANES))],
                                 axis=-1).astype(BF16)
            cv = jnp.concatenate([cache_a_v[:, i].reshape(nb_s, past, aw),
                                  _dup_heads(cache_b_v[:, i].reshape(nb_s, past, LANES))],
                                 axis=-1).astype(BF16)
            o = attention(qs, ks.reshape(nb_s, dec_seq, -1), vs.reshape(nb_s, dec_seq, -1),
                          (ck, cv), lam_row, sub_row, o, batch=nb_s, row0=n_p, n_out=n_tok,
                          n_diff=n_a, out_scale=1.0 - lam_init, tq=256, tk=512)
            x = linear(o, w_o, seg, layer=i, resid=x, mod=mod, gate_row=2, tm=tm, tn=d)
        else:
            j = l // 2
            w_in_t = hy_w_in[j].T
            b_in_col = hy_b_in[j][:, None]
            cwb = jnp.concatenate([hy_conv_w[j].T, hy_conv_b[j][:, None],
                                   jnp.zeros((3 * d, MOD_ROWS - 4), F32)], axis=1)
            sk = jnp.concatenate([hy_bias[j].T, jnp.zeros((d, MOD_ROWS - 2), F32)], axis=1)
            fprm = (hy_f_w1[j], hy_f_b1[j], hy_f_freq1[j], hy_f_w2[j], hy_f_b2[j], hy_f_freq2[j],
                    hy_f_w3[j])
            zs = []
            for row0, nbat, ln, p, td, tl in ((0, nb_p, seq, seq, 256, seq),
                                              (n_p, nb_s, dec_seq, 512, 64, 512)):
                wf, wi, sgn = _dft_tables(p)
                g = _hyena_filter_spectra(ln, p, fprm, wf, sgn, d)
                pt = linear_nt(w_in_t, u, b_in_col, row0=row0, batch=nbat, l=ln, tn=512, tl=tl)
                zt = hyena_conv(pt, cwb, sk, g, wf, wi, p=p, td=td)
                zs.append(jnp.transpose(zt, (0, 2, 1)).reshape(nbat * ln, d).astype(BF16))
            z = jnp.concatenate(zs, axis=0)
            x = linear(z, hy_w_out, seg, layer=j, bias=hy_b_out[:, None, :], resid=x, mod=mod,
                       gate_row=2, tm=tm, tn=d)
        h, e_t, w_t, r_t, cnt = moe_pre(x, norm2_g[l][None, :], mod, seg, router_wt, router_b_col,
                                        tri, shift_row=3, tm=tm)
        tmg = 256
        p_max = 2 * n_tok + n_exp * tmg
        slots, tile_expert, tile_valid = _route(e_t[:2], r_t[:2], cnt[:, 0].astype(I32),
                                                tm=tmg, p_max=p_max)
        xs = moe_dispatch(h, slots, jnp.zeros((p_max, d), F32), td=tm)
        y = moe_ffn(xs, tile_expert, tile_valid, moe_wg, moe_wu, moe_wd, layer=l, tm=tmg)
        combine = functools.partial(moe_combine, y, slots, x, w_t[0][:, None], w_t[1][:, None],
                                    mod, seg_c, gate_row=5, tc=tc)
        if l + 1 < depth:
            x = combine()
        else:
            y_prompt = combine(row0=0, rows=n_p).reshape(nb_p, seq, d)
            y_sample = combine(row0=n_p, rows=n_s).reshape(nb_s, dec_seq, d)

    return (y_prompt, y_sample, jnp.stack(new_ak, axis=1), jnp.stack(new_av, axis=1),
            jnp.stack(new_bk, axis=1), jnp.stack(new_bv, axis=1))
```
